```python
import math
import jax, jax.numpy as jnp
from jax import lax
import numpy as np

D_MODEL = 2048
BATCH = 1
SEQ = 16384
DEPTH = 2

CHUNK = 64
Q_BLOCK = 128
ROPE_THETA = 10000.0
NORM_EPS = 1e-6

DIFF_HEADS = 4
DIFF_HEAD_DIM = 128
DIFF_V_DIM = 2 * DIFF_HEAD_DIM
DIFF_WIDTH = DIFF_HEADS * DIFF_V_DIM

SSD_D_INNER = D_MODEL
SSD_HEAD_DIM = 64
SSD_HEADS = SSD_D_INNER // SSD_HEAD_DIM
SSD_GROUPS = 4
SSD_STATE = 128
SSD_CONV = 4
SSD_CHUNK = CHUNK
SSD_CONV_DIM = SSD_D_INNER + 2 * SSD_GROUPS * SSD_STATE

FOX_HEADS = 8
FOX_HEAD_DIM = 128
FOX_WIDTH = FOX_HEADS * FOX_HEAD_DIM

N_BRANCHES = 3

N_EXPERTS = 64
TOP_K = 6
N_EXPERT_GROUPS = 8
TOPK_GROUPS = 4
D_EXPERT = 512
D_SHARED = 512
ROUTED_SCALE = 2.5
MOE_BLOCK = 128

SPLIT_SIZES = (
    DIFF_HEADS * 2 * DIFF_HEAD_DIM,
    DIFF_HEADS * 2 * DIFF_HEAD_DIM,
    DIFF_WIDTH,
    SSD_D_INNER,
    SSD_CONV_DIM,
    SSD_HEADS,
    FOX_WIDTH,
    FOX_WIDTH,
    FOX_WIDTH,
    FOX_HEADS,
    N_BRANCHES * D_MODEL,
)
IN_PROJ_DIM = sum(SPLIT_SIZES)

kernel_name = 'hybrid_diff_ssd_fox_moe_adaln'


def rms_norm(x, w, eps=NORM_EPS):
    xf = x.astype(jnp.float32)
    y = xf * lax.rsqrt(jnp.mean(xf * xf, axis=-1, keepdims=True) + eps)
    return (y * w.astype(jnp.float32)).astype(x.dtype)


def rope(x, positions):
    d = x.shape[-1]
    half = d // 2
    inv_freq = 1.0 / (ROPE_THETA ** (jnp.arange(half, dtype=jnp.float32) * 2.0 / d))
    ang = positions.astype(jnp.float32)[:, :, None, None] * inv_freq
    cos, sin = jnp.cos(ang), jnp.sin(ang)
    xf = x.astype(jnp.float32)
    x1, x2 = xf[..., :half], xf[..., half:]
    out = jnp.concatenate([x1 * cos - x2 * sin, x2 * cos + x1 * sin], axis=-1)
    return out.astype(x.dtype)


def to_q_blocks(t):
    b, s = t.shape[:2]
    return jnp.swapaxes(t.reshape((b, s // Q_BLOCK, Q_BLOCK) + t.shape[2:]), 0, 1)


def from_q_blocks(t):
    nq, b, qb = t.shape[:3]
    return jnp.swapaxes(t, 0, 1).reshape((b, nq * qb) + t.shape[3:])


def diff_attention(q, k, v, lam_params, subln_w, positions, layer_idx):
    b, s, _ = q.shape
    q = rope(q.reshape(b, s, DIFF_HEADS * 2, DIFF_HEAD_DIM), positions).reshape(b, s, DIFF_HEADS, 2, DIFF_HEAD_DIM)
    k = rope(k.reshape(b, s, DIFF_HEADS * 2, DIFF_HEAD_DIM), positions).reshape(b, s, DIFF_HEADS, 2, DIFF_HEAD_DIM)
    v = v.reshape(b, s, DIFF_HEADS, DIFF_V_DIM)
    lam_init = 0.8 - 0.6 * math.exp(-0.3 * layer_idx)
    lp = lam_params.astype(jnp.float32)
    lam = jnp.exp(jnp.sum(lp[0] * lp[1])) - jnp.exp(jnp.sum(lp[2] * lp[3])) + lam_init
    scale = DIFF_HEAD_DIM ** -0.5
    key_chunk = jnp.arange(s) // CHUNK

    def block(args):
        qb, qid = args
        sc = jnp.einsum('bqhmd,bkhmd->bmhqk', qb, k).astype(jnp.float32) * scale
        mask = key_chunk[None, :] <= (qid // CHUNK)[:, None]
        sc = jnp.where(mask, sc, -jnp.inf)
        p = jax.nn.softmax(sc, axis=-1)
        a = p[:, 0] - lam * p[:, 1]
        return jnp.einsum('bhqk,bkhe->bqhe', a.astype(v.dtype), v)

    qidx = jnp.arange(s).reshape(s // Q_BLOCK, Q_BLOCK)
    o = from_q_blocks(lax.map(block, (to_q_blocks(q), qidx)))
    o = rms_norm(o, subln_w) * (1.0 - lam_init)
    return o.reshape(b, s, DIFF_WIDTH)


def causal_depthwise_conv(x, w, bias):
    ch = x.shape[-1]
    out = lax.conv_general_dilated(x, w[:, None, :].astype(x.dtype), window_strides=(1,),
                                   padding=[(SSD_CONV - 1, 0)],
                                   dimension_numbers=('NWC', 'WIO', 'NWC'),
                                   feature_group_count=ch)
    return out + bias


def ssd_chunked_scan(x, dt, a_head, bm, cm):
    bsz, s, h, p = x.shape
    nc, q, g, r, n = s // SSD_CHUNK, SSD_CHUNK, SSD_GROUPS, SSD_HEADS // SSD_GROUPS, SSD_STATE
    xdt = (x.astype(jnp.float32) * dt[..., None]).reshape(bsz, nc, q, g, r, p)
    a = (dt * a_head).reshape(bsz, nc, q, g, r)
    a_cum = jnp.cumsum(a, axis=2)
    bc = bm.astype(jnp.float32).reshape(bsz, nc, q, g, n)
    cc = cm.astype(jnp.float32).reshape(bsz, nc, q, g, n)
    seg = a_cum[:, :, :, None] - a_cum[:, :, None, :]
    causal = jnp.tril(jnp.ones((q, q), dtype=bool))[None, None, :, :, None, None]
    lmat = jnp.exp(jnp.where(causal, seg, -jnp.inf))
    cb = jnp.einsum('bcqgn,bcsgn->bcqsg', cc, bc)
    y_diag = jnp.einsum('bcqsg,bcqsgr,bcsgrp->bcqgrp', cb, lmat, xdt)
    decay_s = jnp.exp(a_cum[:, :, -1:] - a_cum)
    states = jnp.einsum('bcsgn,bcsgr,bcsgrp->bcgrpn', bc, decay_s, xdt)
    chunk_decay = jnp.exp(a_cum[:, :, -1])

    def step(hstate, inp):
        st, dec = inp
        return hstate * dec[..., None, None] + st, hstate

    h0 = jnp.zeros((bsz, g, r, p, n), jnp.float32)
    _, prev = lax.scan(step, h0, (jnp.swapaxes(states, 0, 1), jnp.swapaxes(chunk_decay, 0, 1)))
    prev = jnp.swapaxes(prev, 0, 1)
    y_off = jnp.einsum('bcqgn,bcgrpn,bcqgr->bcqgrp', cc, prev, jnp.exp(a_cum))
    return (y_diag + y_off).reshape(bsz, s, h, p).astype(x.dtype)


def ssd_mixer(z, xbc, dt, conv_w, conv_b, dt_bias, a_log, d_skip, norm_w):
    b, s, _ = z.shape
    xbc = jax.nn.silu(causal_depthwise_conv(xbc, conv_w, conv_b))
    xs, bm, cm = jnp.split(xbc, [SSD_D_INNER, SSD_D_INNER + SSD_GROUPS * SSD_STATE], axis=-1)
    xs = xs.reshape(b, s, SSD_HEADS, SSD_HEAD_DIM)
    bm = bm.reshape(b, s, SSD_GROUPS, SSD_STATE)
    cm = cm.reshape(b, s, SSD_GROUPS, SSD_STATE)
    dt = jax.nn.softplus(dt.astype(jnp.float32) + dt_bias.astype(jnp.float32))
    a_head = -jnp.exp(a_log.astype(jnp.float32))
    y = ssd_chunked_scan(xs, dt, a_head, bm, cm)
    y = y + d_skip[:, None] * xs
    y = y.reshape(b, s, SSD_D_INNER) * jax.nn.silu(z)
    y = rms_norm(y.reshape(b, s, SSD_GROUPS, SSD_D_INNER // SSD_GROUPS),
                 norm_w.reshape(SSD_GROUPS, SSD_D_INNER // SSD_GROUPS))
    return y.reshape(b, s, SSD_D_INNER)


def forgetting_attention(q, k, v, f_logit, f_bias):
    b, s, _ = q.shape
    q = q.reshape(b, s, FOX_HEADS, FOX_HEAD_DIM)
    k = k.reshape(b, s, FOX_HEADS, FOX_HEAD_DIM)
    v = v.reshape(b, s, FOX_HEADS, FOX_HEAD_DIM)
    log_f = jax.nn.log_sigmoid(f_logit.astype(jnp.float32) + f_bias.astype(jnp.float32))
    cum = jnp.cumsum(log_f, axis=1)
    cum_k = jnp.swapaxes(cum, 1, 2)
    scale = FOX_HEAD_DIM ** -0.5
    key_pos = jnp.arange(s)

    def block(args):
        qb, cq, qid = args
        sc = jnp.einsum('bqhd,bkhd->bhqk', qb, k).astype(jnp.float32) * scale
        sc = sc + jnp.swapaxes(cq, 1, 2)[..., None] - cum_k[:, :, None, :]
        mask = key_pos[None, :] <= qid[:, None]
        sc = jnp.where(mask, sc, -jnp.inf)
        p = jax.nn.softmax(sc, axis=-1)
        return jnp.einsum('bhqk,bkhd->bqhd', p.astype(v.dtype), v)

    qidx = jnp.arange(s).reshape(s // Q_BLOCK, Q_BLOCK)
    o = from_q_blocks(lax.map(block, (to_q_blocks(q), to_q_blocks(cum), qidx)))
    return o.reshape(b, s, FOX_WIDTH)


def hybrid_mixer(h, positions, w_in, diff_lambda, diff_subln_w, ssd_conv_w, ssd_conv_b,
                 ssd_dt_bias, ssd_a_log, ssd_d, ssd_norm_w, fox_f_bias,
                 w_br_diff, w_br_ssd, w_br_fox, w_out, layer_idx):
    b, s, _ = h.shape
    proj = h @ w_in
    cuts = [int(i) for i in np.cumsum(SPLIT_SIZES)[:-1]]
    dq, dk, dv, sz, sxbc, sdt, fq, fk, fv, ff, gates = jnp.split(proj, cuts, axis=-1)
    ya = diff_attention(dq, dk, dv, diff_lambda, diff_subln_w, positions, layer_idx)
    yb = ssd_mixer(sz, sxbc, sdt, ssd_conv_w, ssd_conv_b, ssd_dt_bias, ssd_a_log, ssd_d, ssd_norm_w)
    yc = forgetting_attention(fq, fk, fv, ff, fox_f_bias)
    g = jax.nn.sigmoid(gates.astype(jnp.float32)).astype(h.dtype).reshape(b, s, N_BRANCHES, D_MODEL)
    merged = (g[:, :, 0] * (ya @ w_br_diff) + g[:, :, 1] * (yb @ w_br_ssd)
              + g[:, :, 2] * (yc @ w_br_fox))
    return merged @ w_out


def moe_ffn(h, router_w, router_bias, w_gate, w_up, w_down, ws_gate, ws_up, ws_down):
    b, s, d = h.shape
    t = h.reshape(-1, d)
    n_tok = t.shape[0]
    scores = jax.nn.sigmoid((t @ router_w).astype(jnp.float32))
    biased = scores + router_bias.astype(jnp.float32)
    grouped = biased.reshape(n_tok, N_EXPERT_GROUPS, N_EXPERTS // N_EXPERT_GROUPS)
    group_score = jnp.sum(lax.top_k(grouped, 2)[0], axis=-1)
    _, gidx = lax.top_k(group_score, TOPK_GROUPS)
    gmask = jnp.any(gidx[..., None] == jnp.arange(N_EXPERT_GROUPS), axis=-2)
    emask = jnp.repeat(gmask, N_EXPERTS // N_EXPERT_GROUPS, axis=-1)
    _, eidx = lax.top_k(jnp.where(emask, biased, -jnp.inf), TOP_K)
    wsel = jnp.take_along_axis(scores, eidx, axis=-1)
    wsel = wsel / jnp.sum(wsel, axis=-1, keepdims=True) * ROUTED_SCALE
    combine = jnp.zeros((n_tok, N_EXPERTS), jnp.float32).at[jnp.arange(n_tok)[:, None], eidx].set(wsel)

    def block(args):
        tb, cb = args
        gt = jnp.einsum('td,edf->tef', tb, w_gate)
        up = jnp.einsum('td,edf->tef', tb, w_up)
        act = jax.nn.silu(gt) * up * cb[..., None].astype(tb.dtype)
        return jnp.einsum('tef,efd->td', act, w_down)

    routed = lax.map(block, (t.reshape(-1, MOE_BLOCK, d),
                             combine.reshape(-1, MOE_BLOCK, N_EXPERTS))).reshape(n_tok, d)
    shared = (jax.nn.silu(t @ ws_gate) * (t @ ws_up)) @ ws_down
    return (routed + shared).reshape(b, s, d)


def setup_inputs(seed: int = 0) -> dict:
    key = jax.random.key(seed)
    ks = iter(jax.random.split(key, 48))
    f32 = jnp.float32
    L, D = DEPTH, D_MODEL

    def nrm(shape, scale):
        return jax.random.normal(next(ks), shape, f32) * scale

    def gain(shape):
        return 1.0 + nrm(shape, 0.02)

    x = nrm((BATCH, SEQ, D), 1.0)
    c = nrm((BATCH, D), 1.0)
    offset = jax.random.randint(next(ks), (BATCH, 1), 0, 64, dtype=jnp.int32) * CHUNK
    positions = offset + jnp.arange(SEQ, dtype=jnp.int32)[None, :]
    ada_w = nrm((L, D, 6 * D), 0.5 * D ** -0.5)
    ada_b = nrm((L, 6 * D), 0.01)
    norm_mix_w = gain((L, D))
    norm_ffn_w = gain((L, D))
    w_in = nrm((L, D, IN_PROJ_DIM), D ** -0.5)
    diff_lambda = nrm((L, 4, DIFF_HEAD_DIM), 0.1)
    diff_subln_w = gain((L, DIFF_V_DIM))
    ssd_conv_w = nrm((L, SSD_CONV, SSD_CONV_DIM), SSD_CONV ** -0.5)
    ssd_conv_b = nrm((L, SSD_CONV_DIM), 0.01)
    dt0 = jnp.exp(jax.random.uniform(next(ks), (L, SSD_HEADS), f32, math.log(1e-3), math.log(1e-1)))
    ssd_dt_bias = dt0 + jnp.log(-jnp.expm1(-dt0))
    ssd_a_log = jnp.log(jax.random.uniform(next(ks), (L, SSD_HEADS), f32, 1.0, 16.0))
    ssd_d = gain((L, SSD_HEADS))
    ssd_norm_w = gain((L, SSD_D_INNER))
    fox_f_bias = jax.random.uniform(next(ks), (L, FOX_HEADS), f32, 1.0, 5.0)
    w_br_diff = nrm((L, DIFF_WIDTH, D), DIFF_WIDTH ** -0.5)
    w_br_ssd = nrm((L, SSD_D_INNER, D), SSD_D_INNER ** -0.5)
    w_br_fox = nrm((L, FOX_WIDTH, D), FOX_WIDTH ** -0.5)
    w_out = nrm((L, D, D), D ** -0.5)
    router_w = nrm((L, D, N_EXPERTS), D ** -0.5)
    router_bias = nrm((L, N_EXPERTS), 0.01)
    moe_w_gate = nrm((L, N_EXPERTS, D, D_EXPERT), D ** -0.5)
    moe_w_up = nrm((L, N_EXPERTS, D, D_EXPERT), D ** -0.5)
    moe_w_down = nrm((L, N_EXPERTS, D_EXPERT, D), D_EXPERT ** -0.5)
    shared_w_gate = nrm((L, D, D_SHARED), D ** -0.5)
    shared_w_up = nrm((L, D, D_SHARED), D ** -0.5)
    shared_w_down = nrm((L, D_SHARED, D), D_SHARED ** -0.5)
    final_norm_w = gain((D,))
    return {'x': x, 'c': c, 'positions': positions, 'ada_w': ada_w, 'ada_b': ada_b,
            'norm_mix_w': norm_mix_w, 'norm_ffn_w': norm_ffn_w, 'w_in': w_in,
            'diff_lambda': diff_lambda, 'diff_subln_w': diff_subln_w,
            'ssd_conv_w': ssd_conv_w, 'ssd_conv_b': ssd_conv_b, 'ssd_dt_bias': ssd_dt_bias,
            'ssd_a_log': ssd_a_log, 'ssd_d': ssd_d, 'ssd_norm_w': ssd_norm_w,
            'fox_f_bias': fox_f_bias, 'w_br_diff': w_br_diff, 'w_br_ssd': w_br_ssd,
            'w_br_fox': w_br_fox, 'w_out': w_out, 'router_w': router_w, 'router_bias': router_bias,
            'moe_w_gate': moe_w_gate, 'moe_w_up': moe_w_up, 'moe_w_down': moe_w_down,
            'shared_w_gate': shared_w_gate, 'shared_w_up': shared_w_up,
            'shared_w_down': shared_w_down, 'final_norm_w': final_norm_w}


def reference(x, c, positions, ada_w, ada_b, norm_mix_w, norm_ffn_w, w_in, diff_lambda,
              diff_subln_w, ssd_conv_w, ssd_conv_b, ssd_dt_bias, ssd_a_log, ssd_d, ssd_norm_w,
              fox_f_bias, w_br_diff, w_br_ssd, w_br_fox, w_out, router_w, router_bias,
              moe_w_gate, moe_w_up, moe_w_down, shared_w_gate, shared_w_up, shared_w_down,
              final_norm_w):
    cond = jax.nn.silu(c)
    for l in range(DEPTH):
        mod = (cond @ ada_w[l] + ada_b[l])[:, None, :]
        sh1, sc1, g1, sh2, sc2, g2 = jnp.split(mod, 6, axis=-1)
        h = rms_norm(x, norm_mix_w[l]) * (1.0 + sc1) + sh1
        x = x + g1 * hybrid_mixer(h, positions, w_in[l], diff_lambda[l], diff_subln_w[l],
                                  ssd_conv_w[l], ssd_conv_b[l], ssd_dt_bias[l], ssd_a_log[l],
                                  ssd_d[l], ssd_norm_w[l], fox_f_bias[l], w_br_diff[l],
                                  w_br_ssd[l], w_br_fox[l], w_out[l], l)
        h = rms_norm(x, norm_ffn_w[l]) * (1.0 + sc2) + sh2
        x = x + g2 * moe_ffn(h, router_w[l], router_bias[l], moe_w_gate[l], moe_w_up[l],
                             moe_w_down[l], shared_w_gate[l], shared_w_up[l], shared_w_down[l])
    return rms_norm(x, final_norm_w)
```

```python
import functools
import math

import jax
import jax.numpy as jnp
from jax import lax
from jax.experimental import pallas as pl
from jax.experimental.pallas import tpu as pltpu

F32 = jnp.float32
BF16 = jnp.bfloat16
I32 = jnp.int32

D_MODEL = 2048
DEPTH = 2
CHUNK = 64
ROPE_THETA = 10000.0
NORM_EPS = 1e-6
DIFF_HEADS = 4
HEAD_DIM = 128
DIFF_V_DIM = 2 * HEAD_DIM
DIFF_WIDTH = DIFF_HEADS * DIFF_V_DIM
SSD_D_INNER = D_MODEL
SSD_HEAD_DIM = 64
SSD_HEADS = SSD_D_INNER // SSD_HEAD_DIM
SSD_GROUPS = 4
SSD_STATE = 128
SSD_CONV = 4
SSD_CONV_DIM = SSD_D_INNER + 2 * SSD_GROUPS * SSD_STATE
SSD_GROUP_WIDTH = SSD_D_INNER // SSD_GROUPS
FOX_HEADS = 8
FOX_WIDTH = FOX_HEADS * HEAD_DIM
N_BRANCHES = 3
N_EXPERTS = 64
TOP_K = 6
N_EXPERT_GROUPS = 8
EXPERTS_PER_GROUP = N_EXPERTS // N_EXPERT_GROUPS
TOPK_GROUPS = 4
D_EXPERT = 512
D_SHARED = 512
ROUTED_SCALE = 2.5
LOG2E = math.log2(math.e)

LANES_V7X = 128
SUBLANES_V7X = 8
VMEM_BYTES_V7X = 64 * 1024 * 1024
VMEM_LIMIT_BYTES = VMEM_BYTES_V7X - 8 * 1024 * 1024

NORM_ROWS = 512
MM_ROWS = 1024
MM_COLS = 512
ATTN_BLOCK = 512
SSD_BLOCK = 128
ROUTER_ROWS = 512
MOE_ROWS = 256
GATHER_ROWS = 256
FFN_ROWS = 128
ADALN_COLS = 1024
SMALL_COLS = LANES_V7X
DT_LANE0 = 0
FF_LANE0 = SSD_HEADS


def _cparams(*semantics):
    return pltpu.CompilerParams(dimension_semantics=semantics, vmem_limit_bytes=VMEM_LIMIT_BYTES)


def _silu(v):
    return v * jax.nn.sigmoid(v)


def _softplus(v):
    return jnp.maximum(v, 0.0) + jnp.log1p(jnp.exp(-jnp.abs(v)))


def _adaln_kernel(c_ref, w_ref, b_ref, o_ref):
    cond = _silu(c_ref[...]).astype(BF16)
    o_ref[...] = jnp.dot(cond, w_ref[...].astype(BF16), preferred_element_type=F32) + b_ref[...]


def _adaln(c, ada_w, ada_b):
    n_layers, d, n = ada_w.shape
    c8 = jnp.broadcast_to(c.reshape(1, d), (SUBLANES_V7X, d))
    out = pl.pallas_call(
        _adaln_kernel,
        out_shape=jax.ShapeDtypeStruct((n_layers, SUBLANES_V7X, n), F32),
        grid=(n_layers, n // ADALN_COLS),
        in_specs=[pl.BlockSpec((SUBLANES_V7X, d), lambda l, j: (0, 0)),
                  pl.BlockSpec((None, d, ADALN_COLS), lambda l, j: (l, 0, j)),
                  pl.BlockSpec((None, 1, ADALN_COLS), lambda l, j: (l, 0, j))],
        out_specs=pl.BlockSpec((None, SUBLANES_V7X, ADALN_COLS), lambda l, j: (l, 0, j)),
        compiler_params=_cparams("parallel", "parallel"),
        name="adaln",
    )(c8, ada_w, ada_b.reshape(n_layers, 1, n))
    return out[:, 0:1, :]


def _norm_mod_kernel(x_ref, w_ref, sc_ref, sh_ref, *o_refs):
    x = x_ref[...]
    y = x * lax.rsqrt(jnp.mean(x * x, axis=-1, keepdims=True) + NORM_EPS)
    h = (y * w_ref[...]) * (1.0 + sc_ref[...]) + sh_ref[...]
    for o_ref in o_refs:
        o_ref[...] = h.astype(o_ref.dtype)


def _norm_mod(x, w, scale, shift, out_dtypes):
    s, d = x.shape
    row = pl.BlockSpec((1, d), lambda i: (0, 0))
    blk = pl.BlockSpec((NORM_ROWS, d), lambda i: (i, 0))
    outs = pl.pallas_call(
        _norm_mod_kernel,
        out_shape=[jax.ShapeDtypeStruct((s, d), dt) for dt in out_dtypes],
        grid=(s // NORM_ROWS,),
        in_specs=[blk, row, row, row],
        out_specs=[blk for _ in out_dtypes],
        compiler_params=_cparams("parallel"),
        name="norm_mod",
    )(x, w.reshape(1, d), scale.reshape(1, d), shift.reshape(1, d))
    return outs


def _rope_rotate(v, cos, sin_signed):
    return v * cos + pltpu.roll(v, HEAD_DIM // 2, 1) * sin_signed


def _matmul_kernel(*refs, epilogue):
    a_ref, b_ref = refs[0], refs[1]
    o_ref = refs[-1]
    acc = jnp.dot(a_ref[...], b_ref[...], preferred_element_type=F32)
    if epilogue == "sigmoid":
        acc = jax.nn.sigmoid(acc)
    elif epilogue == "rope":
        cos_ref, sin_ref, scale_ref = refs[2], refs[3], refs[4]
        cos, sin_signed = cos_ref[...], sin_ref[...]
        parts = [_rope_rotate(acc[:, g * HEAD_DIM:(g + 1) * HEAD_DIM], cos, sin_signed)
                 for g in range(acc.shape[1] // HEAD_DIM)]
        acc = jnp.concatenate(parts, axis=1) * scale_ref[...]
    elif epilogue == "colscale":
        acc = acc * refs[2][...]
    elif epilogue == "residual":
        res_ref, gate_ref = refs[2], refs[3]
        acc = res_ref[...] + gate_ref[...] * acc
    o_ref[...] = acc.astype(o_ref.dtype)


def _matmul(a, b, out_dtype, epilogue="none", extra=(), cols=MM_COLS, name="matmul"):
    m, k = a.shape
    n = b.shape[1]
    tn = min(cols, n)
    in_specs = [pl.BlockSpec((MM_ROWS, k), lambda i, j: (i, 0)),
                pl.BlockSpec((k, tn), lambda i, j: (0, j))]
    row_tile = pl.BlockSpec((1, tn), lambda i, j: (0, j))
    if epilogue == "rope":
        tab = pl.BlockSpec((MM_ROWS, HEAD_DIM), lambda i, j: (i, 0))
        in_specs += [tab, tab, row_tile]
    elif epilogue == "colscale":
        in_specs += [row_tile]
    elif epilogue == "residual":
        in_specs += [pl.BlockSpec((MM_ROWS, tn), lambda i, j: (i, j)), row_tile]
    return pl.pallas_call(
        functools.partial(_matmul_kernel, epilogue=epilogue),
        out_shape=jax.ShapeDtypeStruct((m, n), out_dtype),
        grid=(m // MM_ROWS, n // tn),
        in_specs=in_specs,
        out_specs=pl.BlockSpec((MM_ROWS, tn), lambda i, j: (i, j)),
        compiler_params=_cparams("parallel", "parallel"),
        name=name,
    )(a, b, *extra)


def _rope_table_kernel(pos_ref, freq_ref, sign_ref, cos_ref, sin_ref):
    ang = pos_ref[...].astype(F32) * freq_ref[...]
    cos_ref[...] = jnp.cos(ang)
    sin_ref[...] = jnp.sin(ang) * sign_ref[...]


def _rope_tables(positions):
    s = positions.shape[-1]
    half = HEAD_DIM // 2
    inv_freq = 1.0 / (ROPE_THETA ** (jnp.arange(half, dtype=F32) * 2.0 / HEAD_DIM))
    freq = jnp.concatenate([inv_freq, inv_freq]).reshape(1, HEAD_DIM)
    sign = jnp.concatenate([-jnp.ones((half,), F32), jnp.ones((half,), F32)]).reshape(1, HEAD_DIM)
    row = pl.BlockSpec((1, HEAD_DIM), lambda i: (0, 0))
    tab = pl.BlockSpec((NORM_ROWS, HEAD_DIM), lambda i: (i, 0))
    return pl.pallas_call(
        _rope_table_kernel,
        out_shape=[jax.ShapeDtypeStruct((s, HEAD_DIM), F32)] * 2,
        grid=(s // NORM_ROWS,),
        in_specs=[pl.BlockSpec((NORM_ROWS, 1), lambda i: (i, 0)), row, row],
        out_specs=[tab, tab],
        compiler_params=_cparams("parallel"),
        name="rope_tables",
    )(positions.reshape(s, 1), freq, sign)


def _flash_kernel(*refs, block, mask_shift, has_bias):
    if has_bias:
        q_ref, k_ref, v_ref, kb_ref, o_ref = refs
    else:
        q_ref, k_ref, v_ref, o_ref = refs
        kb_ref = None
    i = pl.program_id(1)
    q = q_ref[...]
    dv = v_ref.shape[1]

    def step(j, carry, masked):
        m, l, acc = carry
        start = pl.multiple_of(j * block, block)
        k = k_ref[pl.ds(start, block), :]
        v = v_ref[pl.ds(start, block), :]
        s = lax.dot_general(q, k, (((1,), (1,)), ((), ())), preferred_element_type=F32)
        if has_bias:
            s = s - kb_ref[j]
        if masked:
            row = lax.broadcasted_iota(I32, (block, block), 0) >> mask_shift
            col = lax.broadcasted_iota(I32, (block, block), 1) >> mask_shift
            s = jnp.where(col <= row, s, -jnp.inf)
        m_new = jnp.maximum(m, jnp.max(s, axis=-1, keepdims=True))
        alpha = jnp.exp2(m - m_new)
        p = jnp.exp2(s - m_new)
        l = alpha * l + jnp.sum(p, axis=-1, keepdims=True)
        acc = alpha * acc + jnp.dot(p.astype(BF16), v, preferred_element_type=F32)
        return m_new, l, acc

    init = (jnp.full((block, 1), -jnp.inf, F32), jnp.zeros((block, 1), F32),
            jnp.zeros((block, dv), F32))
    carry = lax.fori_loop(0, i, lambda j, c: step(j, c, False), init)
    _, l, acc = step(i, carry, True)
    o_ref[...] = (acc / l).astype(o_ref.dtype)


def _flash(q_arr, k_arr, v_arr, *, n_heads, q_col0, k_col0, v_col0, dv, v_rep, mask_shift,
           key_bias, out_dtype, name):
    s = q_arr.shape[0]
    blk = ATTN_BLOCK
    qb, kb0, vb0 = q_col0 // HEAD_DIM, k_col0 // HEAD_DIM, v_col0 // dv
    in_specs = [pl.BlockSpec((blk, HEAD_DIM), lambda h, i: (i, qb + h)),
                pl.BlockSpec((s, HEAD_DIM), lambda h, i: (0, kb0 + h)),
                pl.BlockSpec((s, dv), lambda h, i: (0, vb0 + h // v_rep))]
    args = [q_arr, k_arr, v_arr]
    if key_bias is not None:
        in_specs.append(pl.BlockSpec((None, s // blk, 1, blk), lambda h, i: (h, 0, 0, 0)))
        args.append(key_bias)
    return pl.pallas_call(
        functools.partial(_flash_kernel, block=blk, mask_shift=mask_shift,
                          has_bias=key_bias is not None),
        out_shape=jax.ShapeDtypeStruct((s, n_heads * dv), out_dtype),
        grid=(n_heads, s // blk),
        in_specs=in_specs,
        out_specs=pl.BlockSpec((blk, dv), lambda h, i: (i, h)),
        compiler_params=_cparams("parallel", "parallel"),
        name=name,
    )(*args)


def _diff_combine_kernel(o_ref, lam_ref, w_ref, y_ref, *, lam_init):
    lp = lam_ref[...]
    s1 = jnp.sum(lp[0:1] * lp[1:2], axis=-1, keepdims=True)
    s2 = jnp.sum(lp[2:3] * lp[3:4], axis=-1, keepdims=True)
    lam = jnp.exp(s1) - jnp.exp(s2) + lam_init
    w = w_ref[...]
    for h in range(DIFF_HEADS):
        o1 = o_ref[:, (2 * h) * DIFF_V_DIM:(2 * h + 1) * DIFF_V_DIM]
        o2 = o_ref[:, (2 * h + 1) * DIFF_V_DIM:(2 * h + 2) * DIFF_V_DIM]
        d = o1 - lam * o2
        y = d * lax.rsqrt(jnp.mean(d * d, axis=-1, keepdims=True) + NORM_EPS) * w
        y_ref[:, h * DIFF_V_DIM:(h + 1) * DIFF_V_DIM] = (y * (1.0 - lam_init)).astype(y_ref.dtype)


def _diff_combine(o, lam_params, subln_w, layer_idx):
    s = o.shape[0]
    lam_init = 0.8 - 0.6 * math.exp(-0.3 * layer_idx)
    return pl.pallas_call(
        functools.partial(_diff_combine_kernel, lam_init=lam_init),
        out_shape=jax.ShapeDtypeStruct((s, DIFF_WIDTH), BF16),
        grid=(s // NORM_ROWS,),
        in_specs=[pl.BlockSpec((NORM_ROWS, 2 * DIFF_WIDTH), lambda i: (i, 0)),
                  pl.BlockSpec((4, HEAD_DIM), lambda i: (0, 0)),
                  pl.BlockSpec((1, DIFF_V_DIM), lambda i: (0, 0))],
        out_specs=pl.BlockSpec((NORM_ROWS, DIFF_WIDTH), lambda i: (i, 0)),
        compiler_params=_cparams("parallel"),
        name="diff_combine",
    )(o, lam_params, subln_w.reshape(1, DIFF_V_DIM))


def _cumsum_rows(v):
    n = v.shape[0]
    row = lax.broadcasted_iota(I32, (n, 1), 0)
    shift = 1
    while shift < n:
        v = v + jnp.where(row >= shift, pltpu.roll(v, shift, 0), 0.0)
        shift *= 2
    return v


def _expand_heads(v, e_ref):
    hi = v.astype(BF16)
    r1 = v - hi.astype(F32)
    mid = r1.astype(BF16)
    lo = (r1 - mid.astype(F32)).astype(BF16)
    e = e_ref[...]
    return (jnp.dot(hi, e, preferred_element_type=F32) + jnp.dot(mid, e, preferred_element_type=F32)
            + jnp.dot(lo, e, preferred_element_type=F32))


def _ssd_kernel(z_ref, xbc_ref, small_ref, convw_ref, convb_ref, dtb_ref, alog_ref, fb_ref,
                dskip_ref, normw_ref, e64_ref, e128_ref, y_ref, cumt_ref,
                prev_ref, state_ref, fcarry_ref):
    q = SSD_BLOCK
    gw = SSD_GROUP_WIDTH

    @pl.when(pl.program_id(0) == 0)
    def _():
        prev_ref[...] = jnp.zeros_like(prev_ref)
        state_ref[...] = jnp.zeros_like(state_ref)
        fcarry_ref[...] = jnp.zeros_like(fcarry_ref)

    cur = xbc_ref[...]
    prev = prev_ref[...]
    row = lax.broadcasted_iota(I32, (q, 1), 0)
    conv = cur * convw_ref[SSD_CONV - 1:SSD_CONV, :]
    for k in range(1, SSD_CONV):
        shifted = jnp.where(row < k, pltpu.roll(prev, k, 0), pltpu.roll(cur, k, 0))
        conv = conv + shifted * convw_ref[SSD_CONV - 1 - k:SSD_CONV - k, :]
    prev_ref[...] = cur
    xa = _silu(conv + convb_ref[...])
    xs = xa[:, :SSD_D_INNER]

    small = small_ref[...]
    lane = lax.broadcasted_iota(I32, (1, SMALL_COLS), 1)
    is_dt = (lane >= DT_LANE0) & (lane < DT_LANE0 + SSD_HEADS)
    is_ff = (lane >= FF_LANE0) & (lane < FF_LANE0 + FOX_HEADS)

    logf = jnp.where(is_ff, -_softplus(-(small + fb_ref[...])), 0.0)
    cum = _cumsum_rows(logf) + fcarry_ref[...]
    fcarry_ref[...] = cum[q - 1:q, :]
    cumt_ref[...] = cum.T

    dt = jnp.where(is_dt, _softplus(small + dtb_ref[...]), 0.0)
    a = dt * (-jnp.exp(alog_ref[...]))
    acum = _cumsum_rows(a)
    acum_t = acum.T
    acum_e = _expand_heads(acum, e64_ref)
    acum_b = _expand_heads(acum, e128_ref)
    dt_e = _expand_heads(dt, e64_ref)
    atot_e = acum_e[q - 1:q, :]
    xdt = xs * dt_e
    xdt_b = xdt.astype(BF16)
    xd_b = (xdt * jnp.exp(atot_e - acum_e)).astype(BF16)
    eacum = jnp.exp(acum_e)
    etot = jnp.exp(atot_e)

    tril = lax.broadcasted_iota(I32, (q, q), 0) >= lax.broadcasted_iota(I32, (q, q), 1)
    lane_q = lax.broadcasted_iota(I32, (1, LANES_V7X), 1)
    half_masks = (lane_q < SSD_HEAD_DIM, lane_q >= SSD_HEAD_DIM)
    heads_per_group = SSD_HEADS // SSD_GROUPS

    for g in range(SSD_GROUPS):
        gs = slice(g * gw, (g + 1) * gw)
        b0 = SSD_D_INNER + g * SSD_STATE
        c0 = SSD_D_INNER + SSD_GROUPS * SSD_STATE + g * SSD_STATE
        bg = xa[:, b0:b0 + SSD_STATE]
        cg_b = xa[:, c0:c0 + SSD_STATE].astype(BF16)
        cb = lax.dot_general(cg_b, bg.astype(BF16), (((1,), (1,)), ((), ())),
                             preferred_element_type=F32)
        st = state_ref[g]
        y_off = jnp.dot(cg_b, st.astype(BF16), preferred_element_type=F32) * eacum[:, gs]
        state_ref[g] = st * etot[:, gs] + jnp.dot(bg.T.astype(BF16), xd_b[:, gs],
                                                  preferred_element_type=F32)
        bands = []
        for pr in range(heads_per_group // 2):
            c_lo = g * gw + pr * LANES_V7X
            band = xdt_b[:, c_lo:c_lo + LANES_V7X]
            yb = jnp.zeros((q, LANES_V7X), F32)
            for hh in range(2):
                h = g * heads_per_group + 2 * pr + hh
                seg = acum_b[:, h * LANES_V7X:(h + 1) * LANES_V7X] - acum_t[h:h + 1, :]
                lmat = jnp.where(tril, jnp.exp(seg), 0.0)
                rhs = jnp.where(half_masks[hh], band, jnp.zeros_like(band))
                yb = yb + jnp.dot((cb * lmat).astype(BF16), rhs, preferred_element_type=F32)
            bands.append(yb)
        y = jnp.concatenate(bands, axis=1) + y_off + dskip_ref[:, gs] * xs[:, gs]
        y = y * _silu(z_ref[:, gs])
        y = y * lax.rsqrt(jnp.mean(y * y, axis=-1, keepdims=True) + NORM_EPS) * normw_ref[:, gs]
        y_ref[:, gs] = y.astype(y_ref.dtype)


def _ssd(z, xbc, small, conv_w, conv_b, dt_bias, a_log, f_bias, d_skip, norm_w):
    s = z.shape[0]
    q = SSD_BLOCK
    pad_row = lambda v, lane0: jnp.zeros((1, SMALL_COLS), F32).at[0, lane0:lane0 + v.shape[0]].set(v)
    head_of_col64 = jnp.arange(SSD_D_INNER) // SSD_HEAD_DIM
    head_of_col128 = jnp.arange(SSD_HEADS * LANES_V7X) // LANES_V7X
    rows = jnp.arange(LANES_V7X)[:, None]
    e64 = (rows == head_of_col64[None, :]).astype(BF16)
    e128 = (rows == head_of_col128[None, :]).astype(BF16)
    full = lambda shape: pl.BlockSpec(shape, lambda c: (0,) * len(shape))
    return pl.pallas_call(
        _ssd_kernel,
        out_shape=[jax.ShapeDtypeStruct((s, SSD_D_INNER), BF16),
                   jax.ShapeDtypeStruct((LANES_V7X, s), F32)],
        grid=(s // q,),
        in_specs=[pl.BlockSpec((q, SSD_D_INNER), lambda c: (c, 0)),
                  pl.BlockSpec((q, SSD_CONV_DIM), lambda c: (c, 0)),
                  pl.BlockSpec((q, SMALL_COLS), lambda c: (c, 0)),
                  full((SSD_CONV, SSD_CONV_DIM)), full((1, SSD_CONV_DIM)),
                  full((1, SMALL_COLS)), full((1, SMALL_COLS)), full((1, SMALL_COLS)),
                  full((1, SSD_D_INNER)), full((1, SSD_D_INNER)),
                  full((LANES_V7X, SSD_D_INNER)), full((LANES_V7X, SSD_HEADS * LANES_V7X))],
        out_specs=[pl.BlockSpec((q, SSD_D_INNER), lambda c: (c, 0)),
                   pl.BlockSpec((LANES_V7X, q), lambda c: (0, c))],
        scratch_shapes=[pltpu.VMEM((q, SSD_CONV_DIM), F32),
                        pltpu.VMEM((SSD_GROUPS, SSD_STATE, SSD_GROUP_WIDTH), F32),
                        pltpu.VMEM((1, SMALL_COLS), F32)],
        compiler_params=_cparams("arbitrary"),
        name="ssd",
    )(z, xbc, small, conv_w, conv_b.reshape(1, -1), pad_row(dt_bias, DT_LANE0),
      pad_row(a_log, DT_LANE0), pad_row(f_bias, FF_LANE0),
      jnp.repeat(d_skip, SSD_HEAD_DIM).reshape(1, -1), norm_w.reshape(1, -1), e64, e128)


def _merge_kernel(ya_ref, yb_ref, yc_ref, wa_ref, wb_ref, wc_ref, ga_ref, gb_ref, gc_ref, o_ref):
    acc = ga_ref[...].astype(F32) * jnp.dot(ya_ref[...], wa_ref[...], preferred_element_type=F32)
    acc += gb_ref[...].astype(F32) * jnp.dot(yb_ref[...], wb_ref[...], preferred_element_type=F32)
    acc += gc_ref[...].astype(F32) * jnp.dot(yc_ref[...], wc_ref[...], preferred_element_type=F32)
    o_ref[...] = acc.astype(o_ref.dtype)


def _merge(ya, yb, yc, wa, wb, wc, gates):
    s = ya.shape[0]
    d = wa.shape[1]
    tn = MM_COLS
    nb = d // tn
    lhs = lambda width: pl.BlockSpec((MM_ROWS, width), lambda i, j: (i, 0))
    rhs = lambda width: pl.BlockSpec((width, tn), lambda i, j: (0, j))
    gate = lambda b: pl.BlockSpec((MM_ROWS, tn), lambda i, j: (i, b * nb + j))
    return pl.pallas_call(
        _merge_kernel,
        out_shape=jax.ShapeDtypeStruct((s, d), BF16),
        grid=(s // MM_ROWS, nb),
        in_specs=[lhs(ya.shape[1]), lhs(yb.shape[1]), lhs(yc.shape[1]),
                  rhs(wa.shape[0]), rhs(wb.shape[0]), rhs(wc.shape[0]),
                  gate(0), gate(1), gate(2)],
        out_specs=pl.BlockSpec((MM_ROWS, tn), lambda i, j: (i, j)),
        compiler_params=_cparams("parallel", "parallel"),
        name="merge",
    )(ya, yb, yc, wa, wb, wc, gates, gates, gates)


def _first_argmax_rows(v, iota, sentinel):
    mx = jnp.max(v, axis=0, keepdims=True)
    ix = jnp.min(jnp.where(v == mx, iota, sentinel), axis=0, keepdims=True)
    return mx, ix


def _router_kernel(h_ref, rwt_ref, bias_ref, idx_ref, w_ref):
    logits = lax.dot_general(rwt_ref[...], h_ref[...], (((1,), (1,)), ((), ())),
                             preferred_element_type=F32)
    scores = jax.nn.sigmoid(logits)
    biased = scores + bias_ref[...]
    n = logits.shape[1]
    sub = lax.broadcasted_iota(I32, (EXPERTS_PER_GROUP, n), 0)
    group_rows = []
    for g in range(N_EXPERT_GROUPS):
        blk = biased[g * EXPERTS_PER_GROUP:(g + 1) * EXPERTS_PER_GROUP, :]
        top1, i1 = _first_argmax_rows(blk, sub, EXPERTS_PER_GROUP)
        top2 = jnp.max(jnp.where(sub == i1, -jnp.inf, blk), axis=0, keepdims=True)
        group_rows.append(top1 + top2)
    gscore = jnp.concatenate(group_rows, axis=0)
    gi = lax.broadcasted_iota(I32, (N_EXPERT_GROUPS, n), 0)
    gsel = jnp.zeros((N_EXPERT_GROUPS, n), F32)
    for _ in range(TOPK_GROUPS):
        _, ix = _first_argmax_rows(gscore, gi, N_EXPERT_GROUPS)
        hit = gi == ix
        gsel = jnp.where(hit, 1.0, gsel)
        gscore = jnp.where(hit, -jnp.inf, gscore)
    emask = jnp.concatenate(
        [jnp.broadcast_to(gsel[g:g + 1, :], (EXPERTS_PER_GROUP, n)) for g in range(N_EXPERT_GROUPS)],
        axis=0)
    masked = jnp.where(emask > 0.0, biased, -jnp.inf)
    ei = lax.broadcasted_iota(I32, (N_EXPERTS, n), 0)
    idx_rows, w_rows = [], []
    for _ in range(TOP_K):
        _, ix = _first_argmax_rows(masked, ei, N_EXPERTS)
        hit = ei == ix
        w_rows.append(jnp.sum(jnp.where(hit, scores, 0.0), axis=0, keepdims=True))
        idx_rows.append(ix)
        masked = jnp.where(hit, -jnp.inf, masked)
    total = w_rows[0]
    for wk in w_rows[1:]:
        total = total + wk
    pad = SUBLANES_V7X - TOP_K
    idx_ref[...] = jnp.concatenate(idx_rows + [jnp.zeros((pad, n), I32)], axis=0)
    w_ref[...] = jnp.concatenate([wk / total * ROUTED_SCALE for wk in w_rows]
                                 + [jnp.zeros((pad, n), F32)], axis=0)


def _router(h_b, router_w, router_bias):
    t, d = h_b.shape
    out = pl.BlockSpec((SUBLANES_V7X, ROUTER_ROWS), lambda i: (0, i))
    idx, w = pl.pallas_call(
        _router_kernel,
        out_shape=[jax.ShapeDtypeStruct((SUBLANES_V7X, t), I32),
                   jax.ShapeDtypeStruct((SUBLANES_V7X, t), F32)],
        grid=(t // ROUTER_ROWS,),
        in_specs=[pl.BlockSpec((ROUTER_ROWS, d), lambda i: (i, 0)),
                  pl.BlockSpec((N_EXPERTS, d), lambda i: (0, 0)),
                  pl.BlockSpec((N_EXPERTS, 1), lambda i: (0, 0))],
        out_specs=[out, out],
        compiler_params=_cparams("parallel"),
        name="router",
    )(h_b, router_w.T.astype(BF16), router_bias.reshape(N_EXPERTS, 1))
    return idx[:TOP_K].T, w[:TOP_K].T


def _route_layout(eidx, wsel):
    t = eidx.shape[0]
    tm = MOE_ROWS
    n_pairs = t * TOP_K
    n_tiles = n_pairs // tm + N_EXPERTS
    experts = jnp.arange(N_EXPERTS, dtype=I32)
    flat_e = eidx.reshape(-1)
    order = jnp.argsort(flat_e, stable=True).astype(I32)
    onehot = jnp.sum((eidx[:, :, None] == experts).astype(I32), axis=1)
    before = jnp.cumsum(onehot, axis=0) - onehot
    counts = jnp.sum(onehot, axis=0)
    padded = ((counts + tm - 1) // tm) * tm
    off_end = jnp.cumsum(padded)
    off = off_end - padded
    cstart = jnp.cumsum(counts) - counts
    pos = off[eidx] + jnp.take_along_axis(before, eidx, axis=1)
    n_used = off_end[-1] // tm
    tiles = jnp.arange(n_tiles, dtype=I32)
    tile_valid = tiles < n_used
    tile_expert = jnp.minimum(jnp.searchsorted(off_end, tiles * tm, side="right"),
                              N_EXPERTS - 1).astype(I32)
    tile_expert = jnp.where(tile_valid, tile_expert, tile_expert[n_used - 1])
    tile_first = jnp.concatenate([jnp.ones((1,), I32),
                                  (tile_expert[1:] != tile_expert[:-1]).astype(I32)])
    slot = jnp.arange(n_tiles * tm, dtype=I32)
    slot_e = jnp.repeat(tile_expert, tm)
    rank = slot - off[slot_e]
    valid = (rank < counts[slot_e]) & jnp.repeat(tile_valid, tm)
    pair = order[jnp.clip(cstart[slot_e] + rank, 0, n_pairs - 1)]
    src_tok = jnp.where(valid, pair // TOP_K, 0).astype(I32)
    w_sorted = jnp.where(valid, wsel.reshape(-1)[pair], 0.0)
    return src_tok, w_sorted, pos.astype(I32), tile_expert, tile_first, tile_valid.astype(I32)


def _gather_rows_kernel(idx_ref, src_hbm, out_ref, sem):
    rows = out_ref.shape[0]

    def row_copy(src_row, dst_row):
        return pltpu.make_async_copy(src_hbm.at[pl.ds(src_row, 1), :],
                                     out_ref.at[pl.ds(dst_row, 1), :], sem)

    def issue(r, carry):
        row_copy(idx_ref[0, r], r).start()
        return carry

    def drain(r, carry):
        row_copy(0, r).wait()
        return carry

    lax.fori_loop(0, rows, issue, 0)
    lax.fori_loop(0, rows, drain, 0)


def _gather_rows(src, idx):
    n = idx.shape[0]
    width = src.shape[1]
    rows = GATHER_ROWS
    return pl.pallas_call(
        _gather_rows_kernel,
        out_shape=jax.ShapeDtypeStruct((n, width), src.dtype),
        grid=(n // rows,),
        in_specs=[pl.BlockSpec((None, 1, rows), lambda i: (i, 0, 0), memory_space=pltpu.SMEM),
                  pl.BlockSpec(memory_space=pl.ANY)],
        out_specs=pl.BlockSpec((rows, width), lambda i: (i, 0)),
        scratch_shapes=[pltpu.SemaphoreType.DMA(())],
        compiler_params=_cparams("arbitrary"),
        name="gather_rows",
    )(idx.reshape(n // rows, 1, rows), src)


def _moe_group_kernel(te_ref, tf_ref, tv_ref, x_ref, w_ref, wg_ref, wu_ref, wd_ref, y_ref,
                      wg_b, wu_b, wd_b):
    t = pl.program_id(0)

    @pl.when(tf_ref[t] == 1)
    def _():
        wg_b[...] = wg_ref[...].astype(BF16)
        wu_b[...] = wu_ref[...].astype(BF16)
        wd_b[...] = wd_ref[...].astype(BF16)

    @pl.when(tv_ref[t] == 1)
    def _():
        x = x_ref[...].astype(BF16)
        gate = jnp.dot(x, wg_b[...], preferred_element_type=F32)
        up = jnp.dot(x, wu_b[...], preferred_element_type=F32)
        act = _silu(gate) * up * w_ref[...]
        y_ref[...] = jnp.dot(act.astype(BF16), wd_b[...], preferred_element_type=F32)

    @pl.when(tv_ref[t] == 0)
    def _():
        y_ref[...] = jnp.zeros_like(y_ref)


def _moe_grouped(x_sorted, w_sorted, tile_expert, tile_first, tile_valid, w_gate, w_up, w_down, layer):
    p, d = x_sorted.shape
    f = w_gate.shape[3]
    tm = MOE_ROWS
    grid_spec = pltpu.PrefetchScalarGridSpec(
        num_scalar_prefetch=3,
        grid=(p // tm,),
        in_specs=[pl.BlockSpec((tm, d), lambda t, te, tf, tv: (t, 0)),
                  pl.BlockSpec((tm, 1), lambda t, te, tf, tv: (t, 0)),
                  pl.BlockSpec((None, None, d, f), lambda t, te, tf, tv: (layer, te[t], 0, 0)),
                  pl.BlockSpec((None, None, d, f), lambda t, te, tf, tv: (layer, te[t], 0, 0)),
                  pl.BlockSpec((None, None, f, d), lambda t, te, tf, tv: (layer, te[t], 0, 0))],
        out_specs=pl.BlockSpec((tm, d), lambda t, te, tf, tv: (t, 0)),
        scratch_shapes=[pltpu.VMEM((d, f), BF16), pltpu.VMEM((d, f), BF16), pltpu.VMEM((f, d), BF16)],
    )
    return pl.pallas_call(
        _moe_group_kernel,
        out_shape=jax.ShapeDtypeStruct((p, d), F32),
        grid_spec=grid_spec,
        compiler_params=_cparams("arbitrary"),
        name="moe_grouped",
    )(tile_expert, tile_first, tile_valid, x_sorted, w_sorted.reshape(p, 1), w_gate, w_up, w_down)


def _ffn_out_kernel(h_ref, wg_ref, wu_ref, wd_ref, routed_ref, x_ref, gate_ref, o_ref):
    h = h_ref[...]
    act = _silu(jnp.dot(h, wg_ref[...], preferred_element_type=F32)) * jnp.dot(
        h, wu_ref[...], preferred_element_type=F32)
    total = jnp.dot(act.astype(BF16), wd_ref[...], preferred_element_type=F32)
    for k in range(TOP_K):
        total = total + routed_ref[k]
    o_ref[...] = x_ref[...] + gate_ref[...] * total


def _ffn_out(h_b, ws_gate, ws_up, ws_down, routed, x, gate):
    t, d = x.shape
    f = ws_gate.shape[1]
    tm = FFN_ROWS
    blk = pl.BlockSpec((tm, d), lambda i: (i, 0))
    return pl.pallas_call(
        _ffn_out_kernel,
        out_shape=jax.ShapeDtypeStruct((t, d), F32),
        grid=(t // tm,),
        in_specs=[blk,
                  pl.BlockSpec((d, f), lambda i: (0, 0)), pl.BlockSpec((d, f), lambda i: (0, 0)),
                  pl.BlockSpec((f, d), lambda i: (0, 0)),
                  pl.BlockSpec((TOP_K, tm, d), lambda i: (0, i, 0)),
                  blk, pl.BlockSpec((1, d), lambda i: (0, 0))],
        out_specs=blk,
        compiler_params=_cparams("parallel"),
        name="ffn_out",
    )(h_b, ws_gate, ws_up, ws_down, routed, x, gate)


def _in_proj_weights(w_in):
    sizes = (DIFF_WIDTH, DIFF_WIDTH, DIFF_WIDTH, SSD_D_INNER, SSD_CONV_DIM, SSD_HEADS,
             FOX_WIDTH, FOX_WIDTH, FOX_WIDTH, FOX_HEADS, N_BRANCHES * D_MODEL)
    cuts = [0]
    for sz in sizes:
        cuts.append(cuts[-1] + sz)
    seg = lambda a, b: w_in[:, cuts[a]:cuts[b]].astype(BF16)
    small = jnp.zeros((w_in.shape[0], SMALL_COLS), F32)
    small = small.at[:, DT_LANE0:DT_LANE0 + SSD_HEADS].set(w_in[:, cuts[5]:cuts[6]])
    small = small.at[:, FF_LANE0:FF_LANE0 + FOX_HEADS].set(w_in[:, cuts[9]:cuts[10]])
    return dict(diff_qk=seg(0, 2), diff_v=seg(2, 3), ssd_z=seg(3, 4), ssd_xbc=seg(4, 5),
                small=small.astype(BF16), fox=seg(6, 9), gates=seg(10, 11))


def _mixer(x, mod, cos, sin_signed, layer_idx, p):
    s = x.shape[0]
    sh1, sc1, g1 = mod[0], mod[1], mod[2]
    (h,) = _norm_mod(x, p["norm_mix_w"], sc1, sh1, (BF16,))
    w = _in_proj_weights(p["w_in"])
    qscale = HEAD_DIM ** -0.5 * LOG2E

    ones = jnp.ones((1, DIFF_WIDTH), F32)
    diff_qk = _matmul(h, w["diff_qk"], BF16, "rope",
                      (cos, sin_signed, jnp.concatenate([ones * qscale, ones], axis=1)), name="proj_diff_qk")
    diff_v = _matmul(h, w["diff_v"], BF16, name="proj_diff_v")
    ssd_z = _matmul(h, w["ssd_z"], F32, name="proj_ssd_z")
    ssd_xbc = _matmul(h, w["ssd_xbc"], F32, name="proj_ssd_xbc")
    small = _matmul(h, w["small"], F32, name="proj_small")
    fox_scale = jnp.concatenate([jnp.full((1, FOX_WIDTH), qscale, F32),
                                 jnp.ones((1, 2 * FOX_WIDTH), F32)], axis=1)
    fox = _matmul(h, w["fox"], BF16, "colscale", (fox_scale,), name="proj_fox")
    gates = _matmul(h, w["gates"], BF16, "sigmoid", name="proj_gates")

    o_diff = _flash(diff_qk, diff_qk, diff_v, n_heads=2 * DIFF_HEADS, q_col0=0, k_col0=DIFF_WIDTH,
                    v_col0=0, dv=DIFF_V_DIM, v_rep=2, mask_shift=int(math.log2(CHUNK)),
                    key_bias=None, out_dtype=F32, name="diff_attn")
    ya = _diff_combine(o_diff, p["diff_lambda"], p["diff_subln_w"], layer_idx)

    yb, cum_t = _ssd(ssd_z, ssd_xbc, small, p["ssd_conv_w"], p["ssd_conv_b"], p["ssd_dt_bias"],
                     p["ssd_a_log"], p["fox_f_bias"], p["ssd_d"], p["ssd_norm_w"])

    key_bias = (cum_t[FF_LANE0:FF_LANE0 + FOX_HEADS] * LOG2E).reshape(
        FOX_HEADS, s // ATTN_BLOCK, 1, ATTN_BLOCK)
    yc = _flash(fox, fox, fox, n_heads=FOX_HEADS, q_col0=0, k_col0=FOX_WIDTH, v_col0=2 * FOX_WIDTH,
                dv=HEAD_DIM, v_rep=1, mask_shift=0, key_bias=key_bias, out_dtype=BF16,
                name="fox_attn")

    merged = _merge(ya, yb, yc, p["w_br_diff"].astype(BF16), p["w_br_ssd"].astype(BF16),
                    p["w_br_fox"].astype(BF16), gates)
    return _matmul(merged, p["w_out"].astype(BF16), F32, "residual", (x, g1), name="out_proj")


def _moe(x, mod, p, layer, moe_w_gate, moe_w_up, moe_w_down):
    t = x.shape[0]
    sh2, sc2, g2 = mod[3], mod[4], mod[5]
    h_b, h_f = _norm_mod(x, p["norm_ffn_w"], sc2, sh2, (BF16, F32))
    eidx, wsel = _router(h_b, p["router_w"], p["router_bias"])
    src_tok, w_sorted, pos, tile_expert, tile_first, tile_valid = _route_layout(eidx, wsel)
    x_sorted = _gather_rows(h_f, src_tok)
    y_sorted = _moe_grouped(x_sorted, w_sorted, tile_expert, tile_first, tile_valid,
                            moe_w_gate, moe_w_up, moe_w_down, layer)
    routed = _gather_rows(y_sorted, pos.T.reshape(-1)).reshape(TOP_K, t, -1)
    return _ffn_out(h_b, p["shared_w_gate"].astype(BF16), p["shared_w_up"].astype(BF16),
                    p["shared_w_down"].astype(BF16), routed, x, g2)


_LAYER_PARAMS = ("norm_mix_w", "norm_ffn_w", "w_in", "diff_lambda", "diff_subln_w", "ssd_conv_w",
                 "ssd_conv_b", "ssd_dt_bias", "ssd_a_log", "ssd_d", "ssd_norm_w", "fox_f_bias",
                 "w_br_diff", "w_br_ssd", "w_br_fox", "w_out", "router_w", "router_bias",
                 "shared_w_gate", "shared_w_up", "shared_w_down")


def kernel(x, c, positions, ada_w, ada_b, norm_mix_w, norm_ffn_w, w_in, diff_lambda, diff_subln_w, ssd_conv_w, ssd_conv_b, ssd_dt_bias, ssd_a_log, ssd_d, ssd_norm_w, fox_f_bias, w_br_diff, w_br_ssd, w_br_fox, w_out, router_w, router_bias, moe_w_gate, moe_w_up, moe_w_down, shared_w_gate, shared_w_up, shared_w_down, final_norm_w):
    stacked = dict(norm_mix_w=norm_mix_w, norm_ffn_w=norm_ffn_w, w_in=w_in, diff_lambda=diff_lambda,
                   diff_subln_w=diff_subln_w, ssd_conv_w=ssd_conv_w, ssd_conv_b=ssd_conv_b,
                   ssd_dt_bias=ssd_dt_bias, ssd_a_log=ssd_a_log, ssd_d=ssd_d, ssd_norm_w=ssd_norm_w,
                   fox_f_bias=fox_f_bias, w_br_diff=w_br_diff, w_br_ssd=w_br_ssd, w_br_fox=w_br_fox,
                   w_out=w_out, router_w=router_w, router_bias=router_bias,
                   shared_w_gate=shared_w_gate, shared_w_up=shared_w_up, shared_w_down=shared_w_down)
    batch, seq, d = x.shape
    assert batch == 1 and d == D_MODEL and seq % MM_ROWS == 0
    xs = x.reshape(seq, d)
    mods = _adaln(c, ada_w, ada_b)
    cos, sin_signed = _rope_tables(positions)
    for l in range(DEPTH):
        p = {name: stacked[name][l] for name in _LAYER_PARAMS}
        mod = [mods[l, :, i * d:(i + 1) * d] for i in range(6)]
        xs = _mixer(xs, mod, cos, sin_signed, l, p)
        xs = _moe(xs, mod, p, l, moe_w_gate, moe_w_up, moe_w_down)
    zero = jnp.zeros((1, d), F32)
    (out,) = _norm_mod(xs, final_norm_w, zero, zero, (F32,))
    return out.reshape(batch, seq, d)
```

```python
import functools
import math

import jax
import jax.numpy as jnp
from jax import lax
from jax.experimental import pallas as pl
from jax.experimental.pallas import tpu as pltpu

F32 = jnp.float32
BF16 = jnp.bfloat16
I32 = jnp.int32
U32 = jnp.uint32

D_MODEL = 2048
DEPTH = 2
CHUNK = 64
ROPE_THETA = 10000.0
NORM_EPS = 1e-6
DIFF_HEADS = 4
HEAD_DIM = 128
DIFF_V_DIM = 2 * HEAD_DIM
DIFF_WIDTH = DIFF_HEADS * DIFF_V_DIM
SSD_D_INNER = D_MODEL
SSD_HEAD_DIM = 64
SSD_HEADS = SSD_D_INNER // SSD_HEAD_DIM
SSD_GROUPS = 4
SSD_STATE = 128
SSD_CONV = 4
SSD_CONV_DIM = SSD_D_INNER + 2 * SSD_GROUPS * SSD_STATE
SSD_GROUP_WIDTH = SSD_D_INNER // SSD_GROUPS
FOX_HEADS = 8
FOX_WIDTH = FOX_HEADS * HEAD_DIM
N_BRANCHES = 3
N_EXPERTS = 64
TOP_K = 6
N_EXPERT_GROUPS = 8
EXPERTS_PER_GROUP = N_EXPERTS // N_EXPERT_GROUPS
TOPK_GROUPS = 4
D_EXPERT = 512
D_SHARED = 512
ROUTED_SCALE = 2.5
LOG2E = math.log2(math.e)

LANES_V7X = 128
SUBLANES_V7X = 8
VMEM_BYTES_V7X = 64 * 1024 * 1024
VMEM_LIMIT_BYTES = VMEM_BYTES_V7X - 8 * 1024 * 1024

NORM_ROWS = 512
MM_ROWS = 1024
MM_COLS = 512
ATTN_BLOCK = 1024
SSD_BLOCK = 128
ROUTER_ROWS = 512
MOE_ROWS = 256
DISPATCH_ROWS = 256
ADALN_COLS = 1024
SMALL_COLS = LANES_V7X
DT_LANE0 = 0
FF_LANE0 = SSD_HEADS


def _cparams(*semantics):
    return pltpu.CompilerParams(dimension_semantics=semantics, vmem_limit_bytes=VMEM_LIMIT_BYTES)


def _silu(v):
    return v * jax.nn.sigmoid(v)


def _softplus(v):
    return jnp.maximum(v, 0.0) + jnp.log1p(jnp.exp(-jnp.abs(v)))


def _adaln_kernel(c_ref, w_ref, b_ref, o_ref):
    cond = _silu(c_ref[...]).astype(BF16)
    o_ref[...] = jnp.dot(cond, w_ref[...].astype(BF16), preferred_element_type=F32) + b_ref[...]


def _adaln(c, ada_w, ada_b):
    n_layers, d, n = ada_w.shape
    c8 = jnp.broadcast_to(c.reshape(1, d), (SUBLANES_V7X, d))
    out = pl.pallas_call(
        _adaln_kernel,
        out_shape=jax.ShapeDtypeStruct((n_layers, SUBLANES_V7X, n), F32),
        grid=(n_layers, n // ADALN_COLS),
        in_specs=[pl.BlockSpec((SUBLANES_V7X, d), lambda l, j: (0, 0)),
                  pl.BlockSpec((None, d, ADALN_COLS), lambda l, j: (l, 0, j)),
                  pl.BlockSpec((None, 1, ADALN_COLS), lambda l, j: (l, 0, j))],
        out_specs=pl.BlockSpec((None, SUBLANES_V7X, ADALN_COLS), lambda l, j: (l, 0, j)),
        compiler_params=_cparams("parallel", "parallel"),
        name="adaln",
    )(c8, ada_w, ada_b.reshape(n_layers, 1, n))
    return out[:, 0:1, :]


def _norm_mod_kernel(x_ref, w_ref, sc_ref, sh_ref, *o_refs):
    x = x_ref[...]
    y = x * lax.rsqrt(jnp.mean(x * x, axis=-1, keepdims=True) + NORM_EPS)
    h = (y * w_ref[...]) * (1.0 + sc_ref[...]) + sh_ref[...]
    for o_ref in o_refs:
        o_ref[...] = _pack_rows(h) if o_ref.dtype == U32 else h.astype(o_ref.dtype)


def _norm_mod(x, w, scale, shift, out_dtypes):
    s, d = x.shape
    row = pl.BlockSpec((1, d), lambda i: (0, 0))
    blk = pl.BlockSpec((NORM_ROWS, d), lambda i: (i, 0))
    width = lambda dt: d // 2 if dt == U32 else d
    outs = pl.pallas_call(
        _norm_mod_kernel,
        out_shape=[jax.ShapeDtypeStruct((s, width(dt)), dt) for dt in out_dtypes],
        grid=(s // NORM_ROWS,),
        in_specs=[blk, row, row, row],
        out_specs=[pl.BlockSpec((NORM_ROWS, width(dt)), lambda i: (i, 0)) for dt in out_dtypes],
        compiler_params=_cparams("parallel"),
        name="norm_mod",
    )(x, w.reshape(1, d), scale.reshape(1, d), shift.reshape(1, d))
    return outs


def _rope_rotate(v, cos, sin_signed):
    return v * cos + pltpu.roll(v, HEAD_DIM // 2, 1) * sin_signed


def _matmul_kernel(*refs, epilogue):
    a_ref, b_ref = refs[0], refs[1]
    o_ref = refs[-1]
    acc = jnp.dot(a_ref[...], b_ref[...], preferred_element_type=F32)
    if epilogue == "sigmoid":
        acc = jax.nn.sigmoid(acc)
    elif epilogue == "rope":
        cos_ref, sin_ref, scale_ref = refs[2], refs[3], refs[4]
        cos, sin_signed = cos_ref[...], sin_ref[...]
        parts = [_rope_rotate(acc[:, g * HEAD_DIM:(g + 1) * HEAD_DIM], cos, sin_signed)
                 for g in range(acc.shape[1] // HEAD_DIM)]
        acc = jnp.concatenate(parts, axis=1) * scale_ref[...]
    elif epilogue == "colscale":
        acc = acc * refs[2][...]
    elif epilogue == "residual":
        res_ref, gate_ref = refs[2], refs[3]
        acc = res_ref[...] + gate_ref[...] * acc
    o_ref[...] = acc.astype(o_ref.dtype)


def _matmul(a, b, out_dtype, epilogue="none", extra=(), cols=MM_COLS, name="matmul"):
    m, k = a.shape
    n = b.shape[1]
    tn = min(cols, n)
    in_specs = [pl.BlockSpec((MM_ROWS, k), lambda i, j: (i, 0)),
                pl.BlockSpec((k, tn), lambda i, j: (0, j))]
    row_tile = pl.BlockSpec((1, tn), lambda i, j: (0, j))
    if epilogue == "rope":
        tab = pl.BlockSpec((MM_ROWS, HEAD_DIM), lambda i, j: (i, 0))
        in_specs += [tab, tab, row_tile]
    elif epilogue == "colscale":
        in_specs += [row_tile]
    elif epilogue == "residual":
        in_specs += [pl.BlockSpec((MM_ROWS, tn), lambda i, j: (i, j)), row_tile]
    return pl.pallas_call(
        functools.partial(_matmul_kernel, epilogue=epilogue),
        out_shape=jax.ShapeDtypeStruct((m, n), out_dtype),
        grid=(m // MM_ROWS, n // tn),
        in_specs=in_specs,
        out_specs=pl.BlockSpec((MM_ROWS, tn), lambda i, j: (i, j)),
        compiler_params=_cparams("parallel", "parallel"),
        name=name,
    )(a, b, *extra)


def _rope_table_kernel(pos_ref, freq_ref, sign_ref, cos_ref, sin_ref):
    ang = pos_ref[...].astype(F32) * freq_ref[...]
    cos_ref[...] = jnp.cos(ang)
    sin_ref[...] = jnp.sin(ang) * sign_ref[...]


def _rope_tables(positions):
    s = positions.shape[-1]
    half = HEAD_DIM // 2
    inv_freq = 1.0 / (ROPE_THETA ** (jnp.arange(half, dtype=F32) * 2.0 / HEAD_DIM))
    freq = jnp.concatenate([inv_freq, inv_freq]).reshape(1, HEAD_DIM)
    sign = jnp.concatenate([-jnp.ones((half,), F32), jnp.ones((half,), F32)]).reshape(1, HEAD_DIM)
    row = pl.BlockSpec((1, HEAD_DIM), lambda i: (0, 0))
    tab = pl.BlockSpec((NORM_ROWS, HEAD_DIM), lambda i: (i, 0))
    return pl.pallas_call(
        _rope_table_kernel,
        out_shape=[jax.ShapeDtypeStruct((s, HEAD_DIM), F32)] * 2,
        grid=(s // NORM_ROWS,),
        in_specs=[pl.BlockSpec((NORM_ROWS, 1), lambda i: (i, 0)), row, row],
        out_specs=[tab, tab],
        compiler_params=_cparams("parallel"),
        name="rope_tables",
    )(positions.reshape(s, 1), freq, sign)


BIAS_PIECES = 3


def _flash_kernel(*refs, block, mask_shift, has_bias):
    if has_bias:
        q_ref, k_ref, vt_ref, kb_ref, o_ref = refs
    else:
        q_ref, k_ref, vt_ref, o_ref = refs
        kb_ref = None
    i = pl.program_id(1)
    dv = vt_ref.shape[1]
    q = q_ref[...]
    if has_bias:
        lane = lax.broadcasted_iota(I32, (block, HEAD_DIM), 1)
        q = jnp.concatenate([q, jnp.where(lane < BIAS_PIECES, 1.0, 0.0).astype(BF16)], axis=1)

    def step(j, carry, masked):
        m, l, acc = carry
        start = pl.multiple_of(j * block, block)
        k = k_ref[pl.ds(start, block), :]
        if has_bias:
            k = jnp.concatenate([k, kb_ref[pl.ds(start, block), :]], axis=1)
        s = lax.dot_general(k, q, (((1,), (1,)), ((), ())), preferred_element_type=F32)
        if masked:
            key = lax.broadcasted_iota(I32, (block, block), 0) >> mask_shift
            qry = lax.broadcasted_iota(I32, (block, block), 1) >> mask_shift
            s = jnp.where(key <= qry, s, -jnp.inf)
        m_new = jnp.maximum(m, jnp.max(s, axis=0, keepdims=True))
        alpha = jnp.exp2(m - m_new)
        p = jnp.exp2(s - m_new)
        l = alpha * l + jnp.sum(p, axis=0, keepdims=True)
        acc = alpha * acc + jnp.dot(vt_ref[j], p.astype(BF16), preferred_element_type=F32)
        return m_new, l, acc

    init = (jnp.full((1, block), -jnp.inf, F32), jnp.zeros((1, block), F32),
            jnp.zeros((dv, block), F32))
    carry = lax.fori_loop(0, i, lambda j, c: step(j, c, False), init)
    _, l, acc = step(i, carry, True)
    o_ref[...] = (acc / l).T.astype(o_ref.dtype)


def _flash(q_arr, k_arr, v_arr, *, n_heads, q_col0, k_col0, v_col0, dv, v_rep, mask_shift,
           key_bias, out_dtype, name):
    s = q_arr.shape[0]
    blk = ATTN_BLOCK
    nkb = s // blk
    qb, kb0 = q_col0 // HEAD_DIM, k_col0 // HEAD_DIM
    n_vheads = n_heads // v_rep
    v_t = v_arr[:, v_col0:v_col0 + n_vheads * dv].reshape(nkb, blk, n_vheads, dv).transpose(2, 0, 3, 1)
    in_specs = [pl.BlockSpec((blk, HEAD_DIM), lambda h, i: (i, qb + h)),
                pl.BlockSpec((s, HEAD_DIM), lambda h, i: (0, kb0 + h)),
                pl.BlockSpec((None, nkb, dv, blk), lambda h, i: (h // v_rep, 0, 0, 0))]
    args = [q_arr, k_arr, v_t]
    if key_bias is not None:
        in_specs.append(pl.BlockSpec((None, s, HEAD_DIM), lambda h, i: (h, 0, 0)))
        args.append(key_bias)
    return pl.pallas_call(
        functools.partial(_flash_kernel, block=blk, mask_shift=mask_shift,
                          has_bias=key_bias is not None),
        out_shape=jax.ShapeDtypeStruct((s, n_heads * dv), out_dtype),
        grid=(n_heads, nkb),
        in_specs=in_specs,
        out_specs=pl.BlockSpec((blk, dv), lambda h, i: (i, h)),
        compiler_params=_cparams("parallel", "parallel"),
        name=name,
    )(*args)


def _diff_combine_kernel(o_ref, lam_ref, w_ref, y_ref, *, lam_init):
    lp = lam_ref[...]
    s1 = jnp.sum(lp[0:1] * lp[1:2], axis=-1, keepdims=True)
    s2 = jnp.sum(lp[2:3] * lp[3:4], axis=-1, keepdims=True)
    lam = jnp.exp(s1) - jnp.exp(s2) + lam_init
    w = w_ref[...]
    for h in range(DIFF_HEADS):
        o1 = o_ref[:, (2 * h) * DIFF_V_DIM:(2 * h + 1) * DIFF_V_DIM]
        o2 = o_ref[:, (2 * h + 1) * DIFF_V_DIM:(2 * h + 2) * DIFF_V_DIM]
        d = o1 - lam * o2
        y = d * lax.rsqrt(jnp.mean(d * d, axis=-1, keepdims=True) + NORM_EPS) * w
        y_ref[:, h * DIFF_V_DIM:(h + 1) * DIFF_V_DIM] = (y * (1.0 - lam_init)).astype(y_ref.dtype)


def _diff_combine(o, lam_params, subln_w, layer_idx):
    s = o.shape[0]
    lam_init = 0.8 - 0.6 * math.exp(-0.3 * layer_idx)
    return pl.pallas_call(
        functools.partial(_diff_combine_kernel, lam_init=lam_init),
        out_shape=jax.ShapeDtypeStruct((s, DIFF_WIDTH), BF16),
        grid=(s // NORM_ROWS,),
        in_specs=[pl.BlockSpec((NORM_ROWS, 2 * DIFF_WIDTH), lambda i: (i, 0)),
                  pl.BlockSpec((4, HEAD_DIM), lambda i: (0, 0)),
                  pl.BlockSpec((1, DIFF_V_DIM), lambda i: (0, 0))],
        out_specs=pl.BlockSpec((NORM_ROWS, DIFF_WIDTH), lambda i: (i, 0)),
        compiler_params=_cparams("parallel"),
        name="diff_combine",
    )(o, lam_params, subln_w.reshape(1, DIFF_V_DIM))


def _cumsum_rows(v):
    n = v.shape[0]
    row = lax.broadcasted_iota(I32, (n, 1), 0)
    shift = 1
    while shift < n:
        v = v + jnp.where(row >= shift, pltpu.roll(v, shift, 0), 0.0)
        shift *= 2
    return v


def _bf16_bits(v):
    u = pltpu.bitcast(v, U32)
    return (u + jnp.uint32(0x7FFF) + ((u >> 16) & jnp.uint32(1))) & jnp.uint32(0xFFFF0000)


def _split_bf16x3(v):
    hi = pltpu.bitcast(_bf16_bits(v), F32)
    r1 = v - hi
    mid = pltpu.bitcast(_bf16_bits(r1), F32)
    lo = r1 - mid
    return hi.astype(BF16), mid.astype(BF16), lo.astype(BF16)


def _expand_heads(v, e_ref):
    hi, mid, lo = _split_bf16x3(v)
    e = e_ref[...]
    return (jnp.dot(hi, e, preferred_element_type=F32) + jnp.dot(mid, e, preferred_element_type=F32)
            + jnp.dot(lo, e, preferred_element_type=F32))


def _ssd_kernel(z_ref, xbc_ref, small_ref, convw_ref, convb_ref, dtb_ref, alog_ref, fb_ref,
                dskip_ref, normw_ref, e64_ref, e128_ref, y_ref, kb_ref,
                prev_ref, state_ref, fcarry_ref):
    q = SSD_BLOCK
    gw = SSD_GROUP_WIDTH

    @pl.when(pl.program_id(0) == 0)
    def _():
        prev_ref[...] = jnp.zeros_like(prev_ref)
        state_ref[...] = jnp.zeros_like(state_ref)
        fcarry_ref[...] = jnp.zeros_like(fcarry_ref)

    cur = xbc_ref[...]
    prev = prev_ref[...]
    row = lax.broadcasted_iota(I32, (q, 1), 0)
    conv = cur * convw_ref[SSD_CONV - 1:SSD_CONV, :]
    for k in range(1, SSD_CONV):
        shifted = jnp.where(row < k, pltpu.roll(prev, k, 0), pltpu.roll(cur, k, 0))
        conv = conv + shifted * convw_ref[SSD_CONV - 1 - k:SSD_CONV - k, :]
    prev_ref[...] = cur
    xa = _silu(conv + convb_ref[...])
    xs = xa[:, :SSD_D_INNER]

    small = small_ref[...]
    lane = lax.broadcasted_iota(I32, (1, SMALL_COLS), 1)
    is_dt = (lane >= DT_LANE0) & (lane < DT_LANE0 + SSD_HEADS)
    is_ff = (lane >= FF_LANE0) & (lane < FF_LANE0 + FOX_HEADS)

    logf = jnp.where(is_ff, -_softplus(-(small + fb_ref[...])), 0.0)
    cum = _cumsum_rows(logf) + fcarry_ref[...]
    fcarry_ref[...] = cum[q - 1:q, :]
    hi, mid, lo = (v.astype(F32) for v in _split_bf16x3(cum * (-LOG2E)))
    lane_q = lax.broadcasted_iota(I32, (q, LANES_V7X), 1)
    for h in range(FOX_HEADS):
        col = FF_LANE0 + h
        piece = lambda v: jnp.broadcast_to(v[:, col:col + 1], (q, LANES_V7X))
        kb_ref[h] = jnp.where(lane_q == 0, piece(hi),
                              jnp.where(lane_q == 1, piece(mid),
                                        jnp.where(lane_q == 2, piece(lo), 0.0))).astype(BF16)

    dt = jnp.where(is_dt, _softplus(small + dtb_ref[...]), 0.0)
    a = dt * (-jnp.exp(alog_ref[...]))
    acum = _cumsum_rows(a)
    acum_t = acum.T
    acum_e = _expand_heads(acum, e64_ref)
    acum_b = _expand_heads(acum, e128_ref)
    dt_e = _expand_heads(dt, e64_ref)
    atot_e = acum_e[q - 1:q, :]
    xdt = xs * dt_e
    xdt_b = xdt.astype(BF16)
    xd_b = (xdt * jnp.exp(atot_e - acum_e)).astype(BF16)
    eacum = jnp.exp(acum_e)
    etot = jnp.exp(atot_e)

    tril = lax.broadcasted_iota(I32, (q, q), 0) >= lax.broadcasted_iota(I32, (q, q), 1)
    lane_q = lax.broadcasted_iota(I32, (1, LANES_V7X), 1)
    half_masks = (lane_q < SSD_HEAD_DIM, lane_q >= SSD_HEAD_DIM)
    heads_per_group = SSD_HEADS // SSD_GROUPS

    for g in range(SSD_GROUPS):
        gs = slice(g * gw, (g + 1) * gw)
        b0 = SSD_D_INNER + g * SSD_STATE
        c0 = SSD_D_INNER + SSD_GROUPS * SSD_STATE + g * SSD_STATE
        bg = xa[:, b0:b0 + SSD_STATE]
        cg_b = xa[:, c0:c0 + SSD_STATE].astype(BF16)
        cb = lax.dot_general(cg_b, bg.astype(BF16), (((1,), (1,)), ((), ())),
                             preferred_element_type=F32)
        st = state_ref[g]
        y_off = jnp.dot(cg_b, st.astype(BF16), preferred_element_type=F32) * eacum[:, gs]
        state_ref[g] = st * etot[:, gs] + jnp.dot(bg.T.astype(BF16), xd_b[:, gs],
                                                  preferred_element_type=F32)
        bands = []
        for pr in range(heads_per_group // 2):
            c_lo = g * gw + pr * LANES_V7X
            band = xdt_b[:, c_lo:c_lo + LANES_V7X]
            yb = jnp.zeros((q, LANES_V7X), F32)
            for hh in range(2):
                h = g * heads_per_group + 2 * pr + hh
                seg = acum_b[:, h * LANES_V7X:(h + 1) * LANES_V7X] - acum_t[h:h + 1, :]
                lmat = jnp.where(tril, jnp.exp(seg), 0.0)
                rhs = jnp.where(half_masks[hh], band, jnp.zeros_like(band))
                yb = yb + jnp.dot((cb * lmat).astype(BF16), rhs, preferred_element_type=F32)
            bands.append(yb)
        y = jnp.concatenate(bands, axis=1) + y_off + dskip_ref[:, gs] * xs[:, gs]
        y = y * _silu(z_ref[:, gs])
        y = y * lax.rsqrt(jnp.mean(y * y, axis=-1, keepdims=True) + NORM_EPS) * normw_ref[:, gs]
        y_ref[:, gs] = y.astype(y_ref.dtype)


def _ssd(z, xbc, small, conv_w, conv_b, dt_bias, a_log, f_bias, d_skip, norm_w):
    s = z.shape[0]
    q = SSD_BLOCK
    pad_row = lambda v, lane0: jnp.zeros((1, SMALL_COLS), F32).at[0, lane0:lane0 + v.shape[0]].set(v)
    head_of_col64 = jnp.arange(SSD_D_INNER) // SSD_HEAD_DIM
    head_of_col128 = jnp.arange(SSD_HEADS * LANES_V7X) // LANES_V7X
    rows = jnp.arange(LANES_V7X)[:, None]
    e64 = (rows == head_of_col64[None, :]).astype(BF16)
    e128 = (rows == head_of_col128[None, :]).astype(BF16)
    full = lambda shape: pl.BlockSpec(shape, lambda c: (0,) * len(shape))
    return pl.pallas_call(
        _ssd_kernel,
        out_shape=[jax.ShapeDtypeStruct((s, SSD_D_INNER), BF16),
                   jax.ShapeDtypeStruct((FOX_HEADS, s, LANES_V7X), BF16)],
        grid=(s // q,),
        in_specs=[pl.BlockSpec((q, SSD_D_INNER), lambda c: (c, 0)),
                  pl.BlockSpec((q, SSD_CONV_DIM), lambda c: (c, 0)),
                  pl.BlockSpec((q, SMALL_COLS), lambda c: (c, 0)),
                  full((SSD_CONV, SSD_CONV_DIM)), full((1, SSD_CONV_DIM)),
                  full((1, SMALL_COLS)), full((1, SMALL_COLS)), full((1, SMALL_COLS)),
                  full((1, SSD_D_INNER)), full((1, SSD_D_INNER)),
                  full((LANES_V7X, SSD_D_INNER)), full((LANES_V7X, SSD_HEADS * LANES_V7X))],
        out_specs=[pl.BlockSpec((q, SSD_D_INNER), lambda c: (c, 0)),
                   pl.BlockSpec((FOX_HEADS, q, LANES_V7X), lambda c: (0, c, 0))],
        scratch_shapes=[pltpu.VMEM((q, SSD_CONV_DIM), F32),
                        pltpu.VMEM((SSD_GROUPS, SSD_STATE, SSD_GROUP_WIDTH), F32),
                        pltpu.VMEM((1, SMALL_COLS), F32)],
        compiler_params=_cparams("arbitrary"),
        name="ssd",
    )(z, xbc, small, conv_w, conv_b.reshape(1, -1), pad_row(dt_bias, DT_LANE0),
      pad_row(a_log, DT_LANE0), pad_row(f_bias, FF_LANE0),
      jnp.repeat(d_skip, SSD_HEAD_DIM).reshape(1, -1), norm_w.reshape(1, -1), e64, e128)


def _merge_kernel(ya_ref, yb_ref, yc_ref, wa_ref, wb_ref, wc_ref, ga_ref, gb_ref, gc_ref, o_ref):
    acc = ga_ref[...].astype(F32) * jnp.dot(ya_ref[...], wa_ref[...], preferred_element_type=F32)
    acc += gb_ref[...].astype(F32) * jnp.dot(yb_ref[...], wb_ref[...], preferred_element_type=F32)
    acc += gc_ref[...].astype(F32) * jnp.dot(yc_ref[...], wc_ref[...], preferred_element_type=F32)
    o_ref[...] = acc.astype(o_ref.dtype)


def _merge(ya, yb, yc, wa, wb, wc, gates):
    s = ya.shape[0]
    d = wa.shape[1]
    tn = MM_COLS
    nb = d // tn
    lhs = lambda width: pl.BlockSpec((MM_ROWS, width), lambda i, j: (i, 0))
    rhs = lambda width: pl.BlockSpec((width, tn), lambda i, j: (0, j))
    gate = lambda b: pl.BlockSpec((MM_ROWS, tn), lambda i, j: (i, b * nb + j))
    return pl.pallas_call(
        _merge_kernel,
        out_shape=jax.ShapeDtypeStruct((s, d), BF16),
        grid=(s // MM_ROWS, nb),
        in_specs=[lhs(ya.shape[1]), lhs(yb.shape[1]), lhs(yc.shape[1]),
                  rhs(wa.shape[0]), rhs(wb.shape[0]), rhs(wc.shape[0]),
                  gate(0), gate(1), gate(2)],
        out_specs=pl.BlockSpec((MM_ROWS, tn), lambda i, j: (i, j)),
        compiler_params=_cparams("parallel", "parallel"),
        name="merge",
    )(ya, yb, yc, wa, wb, wc, gates, gates, gates)


def _first_argmax_rows(v, iota, sentinel):
    mx = jnp.max(v, axis=0, keepdims=True)
    ix = jnp.min(jnp.where(v == mx, iota, sentinel), axis=0, keepdims=True)
    return mx, ix


def _router_kernel(h_ref, rwt_ref, bias_ref, upper_ref, idx_ref, w_ref, rank_ref, count_ref):
    @pl.when(pl.program_id(0) == 0)
    def _():
        count_ref[...] = jnp.zeros_like(count_ref)

    logits = lax.dot_general(rwt_ref[...], h_ref[...], (((1,), (1,)), ((), ())),
                             preferred_element_type=F32)
    scores = jax.nn.sigmoid(logits)
    biased = scores + bias_ref[...]
    n = logits.shape[1]
    sub = lax.broadcasted_iota(I32, (EXPERTS_PER_GROUP, n), 0)
    group_rows = []
    for g in range(N_EXPERT_GROUPS):
        blk = biased[g * EXPERTS_PER_GROUP:(g + 1) * EXPERTS_PER_GROUP, :]
        top1, i1 = _first_argmax_rows(blk, sub, EXPERTS_PER_GROUP)
        top2 = jnp.max(jnp.where(sub == i1, -jnp.inf, blk), axis=0, keepdims=True)
        group_rows.append(top1 + top2)
    gscore = jnp.concatenate(group_rows, axis=0)
    gi = lax.broadcasted_iota(I32, (N_EXPERT_GROUPS, n), 0)
    gsel = jnp.zeros((N_EXPERT_GROUPS, n), F32)
    for _ in range(TOPK_GROUPS):
        _, ix = _first_argmax_rows(gscore, gi, N_EXPERT_GROUPS)
        hit = gi == ix
        gsel = jnp.where(hit, 1.0, gsel)
        gscore = jnp.where(hit, -jnp.inf, gscore)
    emask = jnp.concatenate(
        [jnp.broadcast_to(gsel[g:g + 1, :], (EXPERTS_PER_GROUP, n)) for g in range(N_EXPERT_GROUPS)],
        axis=0)
    masked = jnp.where(emask > 0.0, biased, -jnp.inf)
    ei = lax.broadcasted_iota(I32, (N_EXPERTS, n), 0)
    idx_rows, w_rows, hits = [], [], []
    for _ in range(TOP_K):
        _, ix = _first_argmax_rows(masked, ei, N_EXPERTS)
        hit = ei == ix
        w_rows.append(jnp.sum(jnp.where(hit, scores, 0.0), axis=0, keepdims=True))
        idx_rows.append(ix)
        hits.append(hit)
        masked = jnp.where(hit, -jnp.inf, masked)
    total = w_rows[0]
    for wk in w_rows[1:]:
        total = total + wk
    pad = SUBLANES_V7X - TOP_K
    idx_ref[...] = jnp.concatenate(idx_rows + [jnp.zeros((pad, n), I32)], axis=0)
    w_ref[...] = jnp.concatenate([wk / total * ROUTED_SCALE for wk in w_rows]
                                 + [jnp.zeros((pad, n), F32)], axis=0)

    chosen = jnp.zeros((N_EXPERTS, n), F32)
    for hit in hits:
        chosen = jnp.where(hit, 1.0, chosen)
    before = jnp.dot(chosen.astype(BF16), upper_ref[...], preferred_element_type=F32)
    before = before + count_ref[:, 0:1]
    rank_rows = [jnp.sum(jnp.where(hit, before, 0.0), axis=0, keepdims=True) for hit in hits]
    rank_ref[...] = jnp.concatenate(rank_rows + [jnp.zeros((pad, n), F32)], axis=0).astype(I32)
    count_ref[...] = count_ref[...] + jnp.sum(chosen, axis=1, keepdims=True)


def _router(h_b, router_w, router_bias):
    t, d = h_b.shape
    rows = ROUTER_ROWS
    out = pl.BlockSpec((SUBLANES_V7X, rows), lambda i: (0, i))
    upper = (jnp.arange(rows)[:, None] < jnp.arange(rows)[None, :]).astype(BF16)
    idx, w, rank, count = pl.pallas_call(
        _router_kernel,
        out_shape=[jax.ShapeDtypeStruct((SUBLANES_V7X, t), I32),
                   jax.ShapeDtypeStruct((SUBLANES_V7X, t), F32),
                   jax.ShapeDtypeStruct((SUBLANES_V7X, t), I32),
                   jax.ShapeDtypeStruct((N_EXPERTS, LANES_V7X), F32)],
        grid=(t // rows,),
        in_specs=[pl.BlockSpec((rows, d), lambda i: (i, 0)),
                  pl.BlockSpec((N_EXPERTS, d), lambda i: (0, 0)),
                  pl.BlockSpec((N_EXPERTS, 1), lambda i: (0, 0)),
                  pl.BlockSpec((rows, rows), lambda i: (0, 0))],
        out_specs=[out, out, out, pl.BlockSpec((N_EXPERTS, LANES_V7X), lambda i: (0, 0))],
        compiler_params=_cparams("arbitrary"),
        name="router",
    )(h_b, router_w.T.astype(BF16), router_bias.reshape(N_EXPERTS, 1), upper)
    return idx[:TOP_K].T, w.T, rank[:TOP_K].T, count[:, 0].astype(I32)


def _route_tiles(counts, n_pairs):
    tm = MOE_ROWS
    n_tiles = n_pairs // tm + N_EXPERTS
    padded = ((counts + tm - 1) // tm) * tm
    off_end = jnp.cumsum(padded)
    off = off_end - padded
    n_used = off_end[-1] // tm
    tiles = jnp.arange(n_tiles, dtype=I32)
    tile_valid = tiles < n_used
    tile_expert = jnp.minimum(jnp.searchsorted(off_end, tiles * tm, side="right"),
                              N_EXPERTS - 1).astype(I32)
    tile_expert = jnp.where(tile_valid, tile_expert, tile_expert[n_used - 1])
    tile_first = jnp.concatenate([jnp.ones((1,), I32),
                                  (tile_expert[1:] != tile_expert[:-1]).astype(I32)])
    return (off.astype(I32), (off + counts).astype(I32), off_end.astype(I32), tile_expert,
            tile_first, tile_valid.astype(I32), jnp.maximum(n_used - 1, 0).astype(I32).reshape(1))


def _pack_rows(v):
    half = v.shape[1] // 2
    return (_bf16_bits(v[:, :half]) >> 16) | _bf16_bits(v[:, half:])


def _unpack_rows(u):
    lo = pltpu.bitcast(u << 16, F32)
    hi = pltpu.bitcast(u & jnp.uint32(0xFFFF0000), F32)
    return jnp.concatenate([lo, hi], axis=1)


def _dispatch_kernel(fill_lo_ref, fill_hi_ref, last_ref, pos_ref, hp_ref, hb_ref, wg_ref, wu_ref,
                     wd_ref, xs_hbm, shared_ref, zero_ref, sem):
    rows = hp_ref.shape[0]
    tile_rows = zero_ref.shape[0]
    n_tiles = xs_hbm.shape[0] // tile_rows

    def row_to_slot(src_row_ref, slot):
        return pltpu.make_async_copy(src_row_ref, xs_hbm.at[pl.ds(slot, 1), :], sem)

    def zeros_to_tile(tile):
        start = pl.multiple_of(tile * tile_rows, tile_rows)
        return pltpu.make_async_copy(zero_ref, xs_hbm.at[pl.ds(start, tile_rows), :], sem)

    @pl.when(pl.program_id(0) == 0)
    def _():
        zero_ref[...] = jnp.zeros_like(zero_ref)
        zero_row = zero_ref.at[pl.ds(0, 1), :]

        def fill_expert(e, carry):
            lax.fori_loop(fill_lo_ref[e], fill_hi_ref[e],
                          lambda s, c: (row_to_slot(zero_row, s).start(), c)[1], 0)
            return carry

        def drain_expert(e, carry):
            lax.fori_loop(fill_lo_ref[e], fill_hi_ref[e],
                          lambda s, c: (row_to_slot(zero_row, s).wait(), c)[1], 0)
            return carry

        lax.fori_loop(0, N_EXPERTS, fill_expert, 0)
        lax.fori_loop(last_ref[0] + 1, n_tiles, lambda tl, c: (zeros_to_tile(tl).start(), c)[1], 0)
        lax.fori_loop(0, N_EXPERTS, drain_expert, 0)
        lax.fori_loop(last_ref[0] + 1, n_tiles, lambda tl, c: (zeros_to_tile(tl).wait(), c)[1], 0)

    def issue(r, carry):
        src = hp_ref.at[pl.ds(r, 1), :]
        for k in range(TOP_K):
            row_to_slot(src, pos_ref[0, r * TOP_K + k]).start()
        return carry

    def drain(r, carry):
        for k in range(TOP_K):
            row_to_slot(hp_ref.at[pl.ds(0, 1), :], 0).wait()
        return carry

    lax.fori_loop(0, rows, issue, 0)
    h = hb_ref[...]
    act = _silu(jnp.dot(h, wg_ref[...], preferred_element_type=F32)) * jnp.dot(
        h, wu_ref[...], preferred_element_type=F32)
    shared_ref[...] = jnp.dot(act.astype(BF16), wd_ref[...], preferred_element_type=F32)
    lax.fori_loop(0, rows, drain, 0)


def _dispatch(h_packed, h_b, pos_tiles, fill_lo, fill_hi, last_tile, ws_gate, ws_up, ws_down, n_slots):
    t, half = h_packed.shape
    d = h_b.shape[1]
    f = ws_gate.shape[1]
    tm = DISPATCH_ROWS
    const = lambda shape: pl.BlockSpec(shape, lambda i, lo, hi, last: (0,) * len(shape))
    grid_spec = pltpu.PrefetchScalarGridSpec(
        num_scalar_prefetch=3,
        grid=(t // tm,),
        in_specs=[pl.BlockSpec((None, 1, tm * TOP_K), lambda i, lo, hi, last: (i, 0, 0),
                               memory_space=pltpu.SMEM),
                  pl.BlockSpec((tm, half), lambda i, lo, hi, last: (i, 0)),
                  pl.BlockSpec((tm, d), lambda i, lo, hi, last: (i, 0)),
                  const((d, f)), const((d, f)), const((f, d))],
        out_specs=[pl.BlockSpec(memory_space=pl.ANY),
                   pl.BlockSpec((tm, d), lambda i, lo, hi, last: (i, 0))],
        scratch_shapes=[pltpu.VMEM((MOE_ROWS, half), U32), pltpu.SemaphoreType.DMA(())],
    )
    return pl.pallas_call(
        _dispatch_kernel,
        out_shape=[jax.ShapeDtypeStruct((n_slots, half), U32), jax.ShapeDtypeStruct((t, d), F32)],
        grid_spec=grid_spec,
        compiler_params=_cparams("arbitrary"),
        name="moe_dispatch",
    )(fill_lo, fill_hi, last_tile, pos_tiles, h_packed, h_b, ws_gate, ws_up, ws_down)


def _moe_group_kernel(te_ref, tf_ref, tv_ref, last_ref, x_ref, wg_ref, wu_ref, wd_ref, y_ref,
                      wg_b, wu_b, wd_b):
    t = pl.program_id(0)

    @pl.when(tf_ref[t] == 1)
    def _():
        wg_b[...] = wg_ref[...].astype(BF16)
        wu_b[...] = wu_ref[...].astype(BF16)
        wd_b[...] = wd_ref[...].astype(BF16)

    @pl.when(tv_ref[t] == 1)
    def _():
        x = _unpack_rows(x_ref[...]).astype(BF16)
        gate = jnp.dot(x, wg_b[...], preferred_element_type=F32)
        up = jnp.dot(x, wu_b[...], preferred_element_type=F32)
        act = _silu(gate) * up
        y_ref[...] = _pack_rows(jnp.dot(act.astype(BF16), wd_b[...], preferred_element_type=F32))

    @pl.when(tv_ref[t] == 0)
    def _():
        y_ref[...] = jnp.zeros_like(y_ref)


def _moe_grouped(x_sorted, tile_expert, tile_first, tile_valid, last_tile, w_gate, w_up, w_down, layer):
    p, half = x_sorted.shape
    d, f = w_gate.shape[2], w_gate.shape[3]
    tm = MOE_ROWS
    grid_spec = pltpu.PrefetchScalarGridSpec(
        num_scalar_prefetch=4,
        grid=(p // tm,),
        in_specs=[pl.BlockSpec((tm, half), lambda t, te, tf, tv, last: (jnp.minimum(t, last[0]), 0)),
                  pl.BlockSpec((None, None, d, f), lambda t, te, tf, tv, last: (layer, te[t], 0, 0)),
                  pl.BlockSpec((None, None, d, f), lambda t, te, tf, tv, last: (layer, te[t], 0, 0)),
                  pl.BlockSpec((None, None, f, d), lambda t, te, tf, tv, last: (layer, te[t], 0, 0))],
        out_specs=pl.BlockSpec((tm, half), lambda t, te, tf, tv, last: (t, 0)),
        scratch_shapes=[pltpu.VMEM((d, f), BF16), pltpu.VMEM((d, f), BF16), pltpu.VMEM((f, d), BF16)],
    )
    return pl.pallas_call(
        _moe_group_kernel,
        out_shape=jax.ShapeDtypeStruct((p, half), U32),
        grid_spec=grid_spec,
        compiler_params=_cparams("arbitrary"),
        name="moe_grouped",
    )(tile_expert, tile_first, tile_valid, last_tile, x_sorted, w_gate, w_up, w_down)


def _combine_kernel(pos_ref, y_hbm, w_ref, shared_ref, x_ref, gate_ref, o_ref, buf_ref, sem):
    rows = x_ref.shape[0]

    def slot_to_row(slot, k, r):
        return pltpu.make_async_copy(y_hbm.at[pl.ds(slot, 1), :], buf_ref.at[k, pl.ds(r, 1), :], sem)

    def issue(r, carry):
        for k in range(TOP_K):
            slot_to_row(pos_ref[0, r * TOP_K + k], k, r).start()
        return carry

    def drain(r, carry):
        for k in range(TOP_K):
            slot_to_row(0, k, r).wait()
        return carry

    lax.fori_loop(0, rows, issue, 0)
    lax.fori_loop(0, rows, drain, 0)
    total = shared_ref[...]
    w = w_ref[...]
    for k in range(TOP_K):
        total = total + w[:, k:k + 1] * _unpack_rows(buf_ref[k])
    o_ref[...] = x_ref[...] + gate_ref[...] * total


def _combine(y_sorted, pos_tiles, wsel, shared, x, gate):
    t, d = x.shape
    half = y_sorted.shape[1]
    tm = DISPATCH_ROWS
    blk = pl.BlockSpec((tm, d), lambda i: (i, 0))
    return pl.pallas_call(
        _combine_kernel,
        out_shape=jax.ShapeDtypeStruct((t, d), F32),
        grid=(t // tm,),
        in_specs=[pl.BlockSpec((None, 1, tm * TOP_K), lambda i: (i, 0, 0), memory_space=pltpu.SMEM),
                  pl.BlockSpec(memory_space=pl.ANY),
                  pl.BlockSpec((tm, SUBLANES_V7X), lambda i: (i, 0)),
                  blk, blk, pl.BlockSpec((1, d), lambda i: (0, 0))],
        out_specs=blk,
        scratch_shapes=[pltpu.VMEM((TOP_K, tm, half), U32), pltpu.SemaphoreType.DMA(())],
        compiler_params=_cparams("arbitrary"),
        name="moe_combine",
    )(pos_tiles, y_sorted, wsel, shared, x, gate)


def _in_proj_weights(w_in):
    sizes = (DIFF_WIDTH, DIFF_WIDTH, DIFF_WIDTH, SSD_D_INNER, SSD_CONV_DIM, SSD_HEADS,
             FOX_WIDTH, FOX_WIDTH, FOX_WIDTH, FOX_HEADS, N_BRANCHES * D_MODEL)
    cuts = [0]
    for sz in sizes:
        cuts.append(cuts[-1] + sz)
    seg = lambda a, b: w_in[:, cuts[a]:cuts[b]].astype(BF16)
    small = jnp.zeros((w_in.shape[0], SMALL_COLS), F32)
    small = small.at[:, DT_LANE0:DT_LANE0 + SSD_HEADS].set(w_in[:, cuts[5]:cuts[6]])
    small = small.at[:, FF_LANE0:FF_LANE0 + FOX_HEADS].set(w_in[:, cuts[9]:cuts[10]])
    return dict(diff_qk=seg(0, 2), diff_v=seg(2, 3), ssd_z=seg(3, 4), ssd_xbc=seg(4, 5),
                small=small.astype(BF16), fox=seg(6, 9), gates=seg(10, 11))


def _mixer(x, mod, cos, sin_signed, layer_idx, p):
    s = x.shape[0]
    sh1, sc1, g1 = mod[0], mod[1], mod[2]
    (h,) = _norm_mod(x, p["norm_mix_w"], sc1, sh1, (BF16,))
    w = _in_proj_weights(p["w_in"])
    qscale = HEAD_DIM ** -0.5 * LOG2E

    ones = jnp.ones((1, DIFF_WIDTH), F32)
    diff_qk = _matmul(h, w["diff_qk"], BF16, "rope",
                      (cos, sin_signed, jnp.concatenate([ones * qscale, ones], axis=1)), name="proj_diff_qk")
    diff_v = _matmul(h, w["diff_v"], BF16, name="proj_diff_v")
    ssd_z = _matmul(h, w["ssd_z"], F32, name="proj_ssd_z")
    ssd_xbc = _matmul(h, w["ssd_xbc"], F32, name="proj_ssd_xbc")
    small = _matmul(h, w["small"], F32, name="proj_small")
    fox_scale = jnp.concatenate([jnp.full((1, FOX_WIDTH), qscale, F32),
                                 jnp.ones((1, 2 * FOX_WIDTH), F32)], axis=1)
    fox = _matmul(h, w["fox"], BF16, "colscale", (fox_scale,), name="proj_fox")
    gates = _matmul(h, w["gates"], BF16, "sigmoid", name="proj_gates")

    o_diff = _flash(diff_qk, diff_qk, diff_v, n_heads=2 * DIFF_HEADS, q_col0=0, k_col0=DIFF_WIDTH,
                    v_col0=0, dv=DIFF_V_DIM, v_rep=2, mask_shift=int(math.log2(CHUNK)),
                    key_bias=None, out_dtype=F32, name="diff_attn")
    ya = _diff_combine(o_diff, p["diff_lambda"], p["diff_subln_w"], layer_idx)

    yb, key_bias = _ssd(ssd_z, ssd_xbc, small, p["ssd_conv_w"], p["ssd_conv_b"], p["ssd_dt_bias"],
                        p["ssd_a_log"], p["fox_f_bias"], p["ssd_d"], p["ssd_norm_w"])

    yc = _flash(fox, fox, fox, n_heads=FOX_HEADS, q_col0=0, k_col0=FOX_WIDTH, v_col0=2 * FOX_WIDTH,
                dv=HEAD_DIM, v_rep=1, mask_shift=0, key_bias=key_bias, out_dtype=BF16,
                name="fox_attn")

    merged = _merge(ya, yb, yc, p["w_br_diff"].astype(BF16), p["w_br_ssd"].astype(BF16),
                    p["w_br_fox"].astype(BF16), gates)
    return _matmul(merged, p["w_out"].astype(BF16), F32, "residual", (x, g1), name="out_proj")


def _moe(x, mod, p, layer, moe_w_gate, moe_w_up, moe_w_down):
    t = x.shape[0]
    sh2, sc2, g2 = mod[3], mod[4], mod[5]
    h_b, h_packed = _norm_mod(x, p["norm_ffn_w"], sc2, sh2, (BF16, U32))
    eidx, wsel, rank, counts = _router(h_b, p["router_w"], p["router_bias"])
    n_pairs = t * TOP_K
    off, fill_lo, fill_hi, tile_expert, tile_first, tile_valid, last_tile = _route_tiles(counts, n_pairs)
    pos_tiles = (off[eidx] + rank).reshape(t // DISPATCH_ROWS, 1, DISPATCH_ROWS * TOP_K)
    n_slots = n_pairs + N_EXPERTS * MOE_ROWS
    x_sorted, shared = _dispatch(h_packed, h_b, pos_tiles, fill_lo, fill_hi, last_tile,
                                 p["shared_w_gate"].astype(BF16), p["shared_w_up"].astype(BF16),
                                 p["shared_w_down"].astype(BF16), n_slots)
    y_sorted = _moe_grouped(x_sorted, tile_expert, tile_first, tile_valid, last_tile,
                            moe_w_gate, moe_w_up, moe_w_down, layer)
    return _combine(y_sorted, pos_tiles, wsel, shared, x, g2)


_LAYER_PARAMS = ("norm_mix_w", "norm_ffn_w", "w_in", "diff_lambda", "diff_subln_w", "ssd_conv_w",
                 "ssd_conv_b", "ssd_dt_bias", "ssd_a_log", "ssd_d", "ssd_norm_w", "fox_f_bias",
                 "w_br_diff", "w_br_ssd", "w_br_fox", "w_out", "router_w", "router_bias",
                 "shared_w_gate", "shared_w_up", "shared_w_down")


def kernel(x, c, positions, ada_w, ada_b, norm_mix_w, norm_ffn_w, w_in, diff_lambda, diff_subln_w, ssd_conv_w, ssd_conv_b, ssd_dt_bias, ssd_a_log, ssd_d, ssd_norm_w, fox_f_bias, w_br_diff, w_br_ssd, w_br_fox, w_out, router_w, router_bias, moe_w_gate, moe_w_up, moe_w_down, shared_w_gate, shared_w_up, shared_w_down, final_norm_w):
    stacked = dict(norm_mix_w=norm_mix_w, norm_ffn_w=norm_ffn_w, w_in=w_in, diff_lambda=diff_lambda,
                   diff_subln_w=diff_subln_w, ssd_conv_w=ssd_conv_w, ssd_conv_b=ssd_conv_b,
                   ssd_dt_bias=ssd_dt_bias, ssd_a_log=ssd_a_log, ssd_d=ssd_d, ssd_norm_w=ssd_norm_w,
                   fox_f_bias=fox_f_bias, w_br_diff=w_br_diff, w_br_ssd=w_br_ssd, w_br_fox=w_br_fox,
                   w_out=w_out, router_w=router_w, router_bias=router_bias,
                   shared_w_gate=shared_w_gate, shared_w_up=shared_w_up, shared_w_down=shared_w_down)
    batch, seq, d = x.shape
    assert batch == 1 and d == D_MODEL and seq % MM_ROWS == 0
    xs = x.reshape(seq, d)
    mods = _adaln(c, ada_w, ada_b)
    cos, sin_signed = _rope_tables(positions)
    for l in range(DEPTH):
        p = {name: stacked[name][l] for name in _LAYER_PARAMS}
        mod = [mods[l, :, i * d:(i + 1) * d] for i in range(6)]
        xs = _mixer(xs, mod, cos, sin_signed, l, p)
        xs = _moe(xs, mod, p, l, moe_w_gate, moe_w_up, moe_w_down)
    zero = jnp.zeros((1, d), F32)
    (out,) = _norm_mod(xs, final_norm_w, zero, zero, (F32,))
    return out.reshape(batch, seq, d)
```

```python
import functools
import math

import jax
import jax.numpy as jnp
from jax import lax
from jax.experimental import pallas as pl
from jax.experimental.pallas import tpu as pltpu

F32 = jnp.float32
BF16 = jnp.bfloat16
I32 = jnp.int32
U32 = jnp.uint32

D_MODEL = 2048
DEPTH = 2
CHUNK = 64
ROPE_THETA = 10000.0
NORM_EPS = 1e-6
DIFF_HEADS = 4
HEAD_DIM = 128
DIFF_V_DIM = 2 * HEAD_DIM
DIFF_WIDTH = DIFF_HEADS * DIFF_V_DIM
SSD_D_INNER = D_MODEL
SSD_HEAD_DIM = 64
SSD_HEADS = SSD_D_INNER // SSD_HEAD_DIM
SSD_GROUPS = 4
SSD_STATE = 128
SSD_CONV = 4
SSD_CONV_DIM = SSD_D_INNER + 2 * SSD_GROUPS * SSD_STATE
SSD_GROUP_WIDTH = SSD_D_INNER // SSD_GROUPS
FOX_HEADS = 8
FOX_WIDTH = FOX_HEADS * HEAD_DIM
N_BRANCHES = 3
N_EXPERTS = 64
TOP_K = 6
N_EXPERT_GROUPS = 8
EXPERTS_PER_GROUP = N_EXPERTS // N_EXPERT_GROUPS
TOPK_GROUPS = 4
D_EXPERT = 512
D_SHARED = 512
ROUTED_SCALE = 2.5
LOG2E = math.log2(math.e)

LANES_V7X = 128
SUBLANES_V7X = 8
VMEM_BYTES_V7X = 64 * 1024 * 1024
VMEM_LIMIT_BYTES = VMEM_BYTES_V7X - 8 * 1024 * 1024

NORM_ROWS = 512
MM_ROWS = 1024
MM_COLS = 512
ATTN_BLOCK = 1024
SSD_BLOCK = 128
ROUTER_ROWS = 512
MOE_ROWS = 256
DISPATCH_ROWS = 256
ADALN_COLS = 1024
SMALL_COLS = LANES_V7X
DT_LANE0 = 0
FF_LANE0 = SSD_HEADS


def _cparams(*semantics):
    return pltpu.CompilerParams(dimension_semantics=semantics, vmem_limit_bytes=VMEM_LIMIT_BYTES)


def _silu(v):
    return v * jax.nn.sigmoid(v)


def _softplus(v):
    return jnp.maximum(v, 0.0) + jnp.log1p(jnp.exp(-jnp.abs(v)))


def _adaln_kernel(c_ref, w_ref, b_ref, o_ref):
    cond = _silu(c_ref[...]).astype(BF16)
    o_ref[...] = jnp.dot(cond, w_ref[...].astype(BF16), preferred_element_type=F32) + b_ref[...]


def _adaln(c, ada_w, ada_b):
    n_layers, d, n = ada_w.shape
    c8 = jnp.broadcast_to(c.reshape(1, d), (SUBLANES_V7X, d))
    out = pl.pallas_call(
        _adaln_kernel,
        out_shape=jax.ShapeDtypeStruct((n_layers, SUBLANES_V7X, n), F32),
        grid=(n_layers, n // ADALN_COLS),
        in_specs=[pl.BlockSpec((SUBLANES_V7X, d), lambda l, j: (0, 0)),
                  pl.BlockSpec((None, d, ADALN_COLS), lambda l, j: (l, 0, j)),
                  pl.BlockSpec((None, 1, ADALN_COLS), lambda l, j: (l, 0, j))],
        out_specs=pl.BlockSpec((None, SUBLANES_V7X, ADALN_COLS), lambda l, j: (l, 0, j)),
        compiler_params=_cparams("parallel", "parallel"),
        name="adaln",
    )(c8, ada_w, ada_b.reshape(n_layers, 1, n))
    return out[:, 0:1, :]


def _norm_mod_kernel(x_ref, w_ref, sc_ref, sh_ref, *o_refs):
    x = x_ref[...]
    y = x * lax.rsqrt(jnp.mean(x * x, axis=-1, keepdims=True) + NORM_EPS)
    h = (y * w_ref[...]) * (1.0 + sc_ref[...]) + sh_ref[...]
    for o_ref in o_refs:
        o_ref[...] = _pack_rows(h) if o_ref.dtype == U32 else h.astype(o_ref.dtype)


def _norm_mod(x, w, scale, shift, out_dtypes):
    s, d = x.shape
    row = pl.BlockSpec((1, d), lambda i: (0, 0))
    blk = pl.BlockSpec((NORM_ROWS, d), lambda i: (i, 0))
    width = lambda dt: d // 2 if dt == U32 else d
    outs = pl.pallas_call(
        _norm_mod_kernel,
        out_shape=[jax.ShapeDtypeStruct((s, width(dt)), dt) for dt in out_dtypes],
        grid=(s // NORM_ROWS,),
        in_specs=[blk, row, row, row],
        out_specs=[pl.BlockSpec((NORM_ROWS, width(dt)), lambda i: (i, 0)) for dt in out_dtypes],
        compiler_params=_cparams("parallel"),
        name="norm_mod",
    )(x, w.reshape(1, d), scale.reshape(1, d), shift.reshape(1, d))
    return outs


def _rope_rotate(v, cos, sin_signed):
    return v * cos + pltpu.roll(v, HEAD_DIM // 2, 1) * sin_signed


def _matmul_kernel(*refs, epilogue):
    a_ref, b_ref = refs[0], refs[1]
    o_ref = refs[-1]
    acc = jnp.dot(a_ref[...], b_ref[...], preferred_element_type=F32)
    if epilogue == "sigmoid":
        acc = jax.nn.sigmoid(acc)
    elif epilogue == "rope":
        cos_ref, sin_ref, scale_ref = refs[2], refs[3], refs[4]
        cos, sin_signed = cos_ref[...], sin_ref[...]
        parts = [_rope_rotate(acc[:, g * HEAD_DIM:(g + 1) * HEAD_DIM], cos, sin_signed)
                 for g in range(acc.shape[1] // HEAD_DIM)]
        acc = jnp.concatenate(parts, axis=1) * scale_ref[...]
    elif epilogue == "colscale":
        acc = acc * refs[2][...]
    elif epilogue == "residual":
        res_ref, gate_ref = refs[2], refs[3]
        acc = res_ref[...] + gate_ref[...] * acc
    o_ref[...] = acc.astype(o_ref.dtype)


def _matmul(a, b, out_dtype, epilogue="none", extra=(), cols=MM_COLS, name="matmul"):
    m, k = a.shape
    n = b.shape[1]
    tn = min(cols, n)
    in_specs = [pl.BlockSpec((MM_ROWS, k), lambda i, j: (i, 0)),
                pl.BlockSpec((k, tn), lambda i, j: (0, j))]
    row_tile = pl.BlockSpec((1, tn), lambda i, j: (0, j))
    if epilogue == "rope":
        tab = pl.BlockSpec((MM_ROWS, HEAD_DIM), lambda i, j: (i, 0))
        in_specs += [tab, tab, row_tile]
    elif epilogue == "colscale":
        in_specs += [row_tile]
    elif epilogue == "residual":
        in_specs += [pl.BlockSpec((MM_ROWS, tn), lambda i, j: (i, j)), row_tile]
    return pl.pallas_call(
        functools.partial(_matmul_kernel, epilogue=epilogue),
        out_shape=jax.ShapeDtypeStruct((m, n), out_dtype),
        grid=(m // MM_ROWS, n // tn),
        in_specs=in_specs,
        out_specs=pl.BlockSpec((MM_ROWS, tn), lambda i, j: (i, j)),
        compiler_params=_cparams("parallel", "parallel"),
        name=name,
    )(a, b, *extra)


def _rope_table_kernel(pos_ref, freq_ref, sign_ref, cos_ref, sin_ref):
    ang = pos_ref[...].astype(F32) * freq_ref[...]
    cos_ref[...] = jnp.cos(ang)
    sin_ref[...] = jnp.sin(ang) * sign_ref[...]


def _rope_tables(positions):
    s = positions.shape[-1]
    half = HEAD_DIM // 2
    inv_freq = 1.0 / (ROPE_THETA ** (jnp.arange(half, dtype=F32) * 2.0 / HEAD_DIM))
    freq = jnp.concatenate([inv_freq, inv_freq]).reshape(1, HEAD_DIM)
    sign = jnp.concatenate([-jnp.ones((half,), F32), jnp.ones((half,), F32)]).reshape(1, HEAD_DIM)
    row = pl.BlockSpec((1, HEAD_DIM), lambda i: (0, 0))
    tab = pl.BlockSpec((NORM_ROWS, HEAD_DIM), lambda i: (i, 0))
    return pl.pallas_call(
        _rope_table_kernel,
        out_shape=[jax.ShapeDtypeStruct((s, HEAD_DIM), F32)] * 2,
        grid=(s // NORM_ROWS,),
        in_specs=[pl.BlockSpec((NORM_ROWS, 1), lambda i: (i, 0)), row, row],
        out_specs=[tab, tab],
        compiler_params=_cparams("parallel"),
        name="rope_tables",
    )(positions.reshape(s, 1), freq, sign)


BIAS_PIECES = 3


def _flash_kernel(*refs, block, mask_shift, has_bias):
    if has_bias:
        q_ref, k_ref, vt_ref, kb_ref, o_ref, s_scr, m_scr, l_scr, acc_scr = refs
    else:
        q_ref, k_ref, vt_ref, o_ref, s_scr, m_scr, l_scr, acc_scr = refs
        kb_ref = None
    i = pl.program_id(1)
    q = q_ref[...]
    if has_bias:
        lane = lax.broadcasted_iota(I32, (block, HEAD_DIM), 1)
        q = jnp.concatenate([q, jnp.where(lane < BIAS_PIECES, 1.0, 0.0).astype(BF16)], axis=1)

    def scores(j, slot):
        start = pl.multiple_of(j * block, block)
        k = k_ref[pl.ds(start, block), :]
        if has_bias:
            k = jnp.concatenate([k, kb_ref[pl.ds(start, block), :]], axis=1)
        s_scr[slot] = lax.dot_general(k, q, (((1,), (1,)), ((), ())),
                                      preferred_element_type=F32)

    def update(j, slot, masked):
        s = s_scr[slot]
        if masked:
            key = lax.broadcasted_iota(I32, (block, block), 0) >> mask_shift
            qry = lax.broadcasted_iota(I32, (block, block), 1) >> mask_shift
            s = jnp.where(key <= qry, s, -jnp.inf)
        m = m_scr[...]
        m_new = jnp.maximum(m, jnp.max(s, axis=0, keepdims=True))
        alpha = jnp.exp2(m - m_new)
        p = jnp.exp2(s - m_new)
        l_scr[...] = alpha * l_scr[...] + jnp.sum(p, axis=0, keepdims=True)
        acc_scr[...] = alpha * acc_scr[...] + jnp.dot(vt_ref[j], p.astype(BF16),
                                                      preferred_element_type=F32)
        m_scr[...] = m_new

    m_scr[...] = jnp.full(m_scr.shape, -jnp.inf, F32)
    l_scr[...] = jnp.zeros(l_scr.shape, F32)
    acc_scr[...] = jnp.zeros(acc_scr.shape, F32)

    scores(0, 0)

    def pair(jj, carry):
        j = 2 * jj
        scores(j + 1, 1)
        update(j, 0, False)
        scores(j + 2, 0)
        update(j + 1, 1, False)
        return carry

    lax.fori_loop(0, i // 2, pair, 0)

    @pl.when(i % 2 == 0)
    def _():
        update(i, 0, True)

    @pl.when(i % 2 == 1)
    def _():
        scores(i, 1)
        update(i - 1, 0, False)
        update(i, 1, True)

    o_ref[...] = (acc_scr[...] / l_scr[...]).T.astype(o_ref.dtype)


def _flash(q_arr, k_arr, v_arr, *, n_heads, q_col0, k_col0, v_col0, dv, v_rep, mask_shift,
           key_bias, out_dtype, name):
    s = q_arr.shape[0]
    blk = ATTN_BLOCK
    nkb = s // blk
    qb, kb0 = q_col0 // HEAD_DIM, k_col0 // HEAD_DIM
    n_vheads = n_heads // v_rep
    v_t = v_arr[:, v_col0:v_col0 + n_vheads * dv].reshape(nkb, blk, n_vheads, dv).transpose(2, 0, 3, 1)
    resident = dict(pipeline_mode=pl.Buffered(1))
    in_specs = [pl.BlockSpec((blk, HEAD_DIM), lambda h, i: (i, qb + h)),
                pl.BlockSpec((s, HEAD_DIM), lambda h, i: (0, kb0 + h), **resident),
                pl.BlockSpec((None, nkb, dv, blk), lambda h, i: (h // v_rep, 0, 0, 0), **resident)]
    args = [q_arr, k_arr, v_t]
    if key_bias is not None:
        in_specs.append(pl.BlockSpec((None, s, HEAD_DIM), lambda h, i: (h, 0, 0), **resident))
        args.append(key_bias)
    return pl.pallas_call(
        functools.partial(_flash_kernel, block=blk, mask_shift=mask_shift,
                          has_bias=key_bias is not None),
        out_shape=jax.ShapeDtypeStruct((s, n_heads * dv), out_dtype),
        grid=(n_heads, nkb),
        in_specs=in_specs,
        out_specs=pl.BlockSpec((blk, dv), lambda h, i: (i, h)),
        scratch_shapes=[pltpu.VMEM((2, blk, blk), F32), pltpu.VMEM((1, blk), F32),
                        pltpu.VMEM((1, blk), F32), pltpu.VMEM((dv, blk), F32)],
        compiler_params=_cparams("parallel", "arbitrary"),
        name=name,
    )(*args)


def _diff_combine_kernel(o_ref, lam_ref, w_ref, y_ref, *, lam_init):
    lp = lam_ref[...]
    s1 = jnp.sum(lp[0:1] * lp[1:2], axis=-1, keepdims=True)
    s2 = jnp.sum(lp[2:3] * lp[3:4], axis=-1, keepdims=True)
    lam = jnp.exp(s1) - jnp.exp(s2) + lam_init
    w = w_ref[...]
    for h in range(DIFF_HEADS):
        o1 = o_ref[:, (2 * h) * DIFF_V_DIM:(2 * h + 1) * DIFF_V_DIM]
        o2 = o_ref[:, (2 * h + 1) * DIFF_V_DIM:(2 * h + 2) * DIFF_V_DIM]
        d = o1 - lam * o2
        y = d * lax.rsqrt(jnp.mean(d * d, axis=-1, keepdims=True) + NORM_EPS) * w
        y_ref[:, h * DIFF_V_DIM:(h + 1) * DIFF_V_DIM] = (y * (1.0 - lam_init)).astype(y_ref.dtype)


def _diff_combine(o, lam_params, subln_w, layer_idx):
    s = o.shape[0]
    lam_init = 0.8 - 0.6 * math.exp(-0.3 * layer_idx)
    return pl.pallas_call(
        functools.partial(_diff_combine_kernel, lam_init=lam_init),
        out_shape=jax.ShapeDtypeStruct((s, DIFF_WIDTH), BF16),
        grid=(s // NORM_ROWS,),
        in_specs=[pl.BlockSpec((NORM_ROWS, 2 * DIFF_WIDTH), lambda i: (i, 0)),
                  pl.BlockSpec((4, HEAD_DIM), lambda i: (0, 0)),
                  pl.BlockSpec((1, DIFF_V_DIM), lambda i: (0, 0))],
        out_specs=pl.BlockSpec((NORM_ROWS, DIFF_WIDTH), lambda i: (i, 0)),
        compiler_params=_cparams("parallel"),
        name="diff_combine",
    )(o, lam_params, subln_w.reshape(1, DIFF_V_DIM))


def _cumsum_rows(v):
    n = v.shape[0]
    row = lax.broadcasted_iota(I32, (n, 1), 0)
    shift = 1
    while shift < n:
        v = v + jnp.where(row >= shift, pltpu.roll(v, shift, 0), 0.0)
        shift *= 2
    return v


def _bf16_bits(v):
    u = pltpu.bitcast(v, U32)
    return (u + jnp.uint32(0x7FFF) + ((u >> 16) & jnp.uint32(1))) & jnp.uint32(0xFFFF0000)


def _split_bf16x3(v):
    hi = pltpu.bitcast(_bf16_bits(v), F32)
    r1 = v - hi
    mid = pltpu.bitcast(_bf16_bits(r1), F32)
    lo = r1 - mid
    return hi.astype(BF16), mid.astype(BF16), lo.astype(BF16)


def _expand_heads(v, e_ref):
    hi, mid, lo = _split_bf16x3(v)
    e = e_ref[...]
    return (jnp.dot(hi, e, preferred_element_type=F32) + jnp.dot(mid, e, preferred_element_type=F32)
            + jnp.dot(lo, e, preferred_element_type=F32))


def _ssd_kernel(z_ref, xbc_ref, small_ref, convw_ref, convb_ref, dtb_ref, alog_ref, fb_ref,
                dskip_ref, normw_ref, e64_ref, e128_ref, y_ref, kb_ref,
                prev_ref, state_ref, fcarry_ref):
    q = SSD_BLOCK
    gw = SSD_GROUP_WIDTH

    @pl.when(pl.program_id(0) == 0)
    def _():
        prev_ref[...] = jnp.zeros_like(prev_ref)
        state_ref[...] = jnp.zeros_like(state_ref)
        fcarry_ref[...] = jnp.zeros_like(fcarry_ref)

    cur = xbc_ref[...]
    prev = prev_ref[...]
    row = lax.broadcasted_iota(I32, (q, 1), 0)
    conv = cur * convw_ref[SSD_CONV - 1:SSD_CONV, :]
    for k in range(1, SSD_CONV):
        shifted = jnp.where(row < k, pltpu.roll(prev, k, 0), pltpu.roll(cur, k, 0))
        conv = conv + shifted * convw_ref[SSD_CONV - 1 - k:SSD_CONV - k, :]
    prev_ref[...] = cur
    xa = _silu(conv + convb_ref[...])
    xs = xa[:, :SSD_D_INNER]

    small = small_ref[...]
    lane = lax.broadcasted_iota(I32, (1, SMALL_COLS), 1)
    is_dt = (lane >= DT_LANE0) & (lane < DT_LANE0 + SSD_HEADS)
    is_ff = (lane >= FF_LANE0) & (lane < FF_LANE0 + FOX_HEADS)

    logf = jnp.where(is_ff, -_softplus(-(small + fb_ref[...])), 0.0)
    cum = _cumsum_rows(logf) + fcarry_ref[...]
    fcarry_ref[...] = cum[q - 1:q, :]
    hi, mid, lo = (v.astype(F32) for v in _split_bf16x3(cum * (-LOG2E)))
    lane_q = lax.broadcasted_iota(I32, (q, LANES_V7X), 1)
    for h in range(FOX_HEADS):
        col = FF_LANE0 + h
        piece = lambda v: jnp.broadcast_to(v[:, col:col + 1], (q, LANES_V7X))
        kb_ref[h] = jnp.where(lane_q == 0, piece(hi),
                              jnp.where(lane_q == 1, piece(mid),
                                        jnp.where(lane_q == 2, piece(lo), 0.0))).astype(BF16)

    dt = jnp.where(is_dt, _softplus(small + dtb_ref[...]), 0.0)
    a = dt * (-jnp.exp(alog_ref[...]))
    acum = _cumsum_rows(a)
    acum_t = acum.T
    acum_e = _expand_heads(acum, e64_ref)
    acum_b = _expand_heads(acum, e128_ref)
    dt_e = _expand_heads(dt, e64_ref)
    atot_e = acum_e[q - 1:q, :]
    xdt = xs * dt_e
    xdt_b = xdt.astype(BF16)
    xd_b = (xdt * jnp.exp(atot_e - acum_e)).astype(BF16)
    eacum = jnp.exp(acum_e)
    etot = jnp.exp(atot_e)

    tril = lax.broadcasted_iota(I32, (q, q), 0) >= lax.broadcasted_iota(I32, (q, q), 1)
    lane_q = lax.broadcasted_iota(I32, (1, LANES_V7X), 1)
    half_masks = (lane_q < SSD_HEAD_DIM, lane_q >= SSD_HEAD_DIM)
    heads_per_group = SSD_HEADS // SSD_GROUPS

    for g in range(SSD_GROUPS):
        gs = slice(g * gw, (g + 1) * gw)
        b0 = SSD_D_INNER + g * SSD_STATE
        c0 = SSD_D_INNER + SSD_GROUPS * SSD_STATE + g * SSD_STATE
        bg = xa[:, b0:b0 + SSD_STATE]
        cg_b = xa[:, c0:c0 + SSD_STATE].astype(BF16)
        cb = lax.dot_general(cg_b, bg.astype(BF16), (((1,), (1,)), ((), ())),
                             preferred_element_type=F32)
        st = state_ref[g]
        y_off = jnp.dot(cg_b, st.astype(BF16), preferred_element_type=F32) * eacum[:, gs]
        state_ref[g] = st * etot[:, gs] + jnp.dot(bg.T.astype(BF16), xd_b[:, gs],
                                                  preferred_element_type=F32)
        bands = []
        for pr in range(heads_per_group // 2):
            c_lo = g * gw + pr * LANES_V7X
            band = xdt_b[:, c_lo:c_lo + LANES_V7X]
            yb = jnp.zeros((q, LANES_V7X), F32)
            for hh in range(2):
                h = g * heads_per_group + 2 * pr + hh
                seg = acum_b[:, h * LANES_V7X:(h + 1) * LANES_V7X] - acum_t[h:h + 1, :]
                lmat = jnp.where(tril, jnp.exp(seg), 0.0)
                rhs = jnp.where(half_masks[hh], band, jnp.zeros_like(band))
                yb = yb + jnp.dot((cb * lmat).astype(BF16), rhs, preferred_element_type=F32)
            bands.append(yb)
        y = jnp.concatenate(bands, axis=1) + y_off + dskip_ref[:, gs] * xs[:, gs]
        y = y * _silu(z_ref[:, gs])
        y = y * lax.rsqrt(jnp.mean(y * y, axis=-1, keepdims=True) + NORM_EPS) * normw_ref[:, gs]
        y_ref[:, gs] = y.astype(y_ref.dtype)


def _ssd(z, xbc, small, conv_w, conv_b, dt_bias, a_log, f_bias, d_skip, norm_w):
    s = z.shape[0]
    q = SSD_BLOCK
    pad_row = lambda v, lane0: jnp.zeros((1, SMALL_COLS), F32).at[0, lane0:lane0 + v.shape[0]].set(v)
    head_of_col64 = jnp.arange(SSD_D_INNER) // SSD_HEAD_DIM
    head_of_col128 = jnp.arange(SSD_HEADS * LANES_V7X) // LANES_V7X
    rows = jnp.arange(LANES_V7X)[:, None]
    e64 = (rows == head_of_col64[None, :]).astype(BF16)
    e128 = (rows == head_of_col128[None, :]).astype(BF16)
    full = lambda shape: pl.BlockSpec(shape, lambda c: (0,) * len(shape))
    return pl.pallas_call(
        _ssd_kernel,
        out_shape=[jax.ShapeDtypeStruct((s, SSD_D_INNER), BF16),
                   jax.ShapeDtypeStruct((FOX_HEADS, s, LANES_V7X), BF16)],
        grid=(s // q,),
        in_specs=[pl.BlockSpec((q, SSD_D_INNER), lambda c: (c, 0)),
                  pl.BlockSpec((q, SSD_CONV_DIM), lambda c: (c, 0)),
                  pl.BlockSpec((q, SMALL_COLS), lambda c: (c, 0)),
                  full((SSD_CONV, SSD_CONV_DIM)), full((1, SSD_CONV_DIM)),
                  full((1, SMALL_COLS)), full((1, SMALL_COLS)), full((1, SMALL_COLS)),
                  full((1, SSD_D_INNER)), full((1, SSD_D_INNER)),
                  full((LANES_V7X, SSD_D_INNER)), full((LANES_V7X, SSD_HEADS * LANES_V7X))],
        out_specs=[pl.BlockSpec((q, SSD_D_INNER), lambda c: (c, 0)),
                   pl.BlockSpec((FOX_HEADS, q, LANES_V7X), lambda c: (0, c, 0))],
        scratch_shapes=[pltpu.VMEM((q, SSD_CONV_DIM), F32),
                        pltpu.VMEM((SSD_GROUPS, SSD_STATE, SSD_GROUP_WIDTH), F32),
                        pltpu.VMEM((1, SMALL_COLS), F32)],
        compiler_params=_cparams("arbitrary"),
        name="ssd",
    )(z, xbc, small, conv_w, conv_b.reshape(1, -1), pad_row(dt_bias, DT_LANE0),
      pad_row(a_log, DT_LANE0), pad_row(f_bias, FF_LANE0),
      jnp.repeat(d_skip, SSD_HEAD_DIM).reshape(1, -1), norm_w.reshape(1, -1), e64, e128)


def _merge_kernel(ya_ref, yb_ref, yc_ref, wa_ref, wb_ref, wc_ref, ga_ref, gb_ref, gc_ref, o_ref):
    acc = ga_ref[...].astype(F32) * jnp.dot(ya_ref[...], wa_ref[...], preferred_element_type=F32)
    acc += gb_ref[...].astype(F32) * jnp.dot(yb_ref[...], wb_ref[...], preferred_element_type=F32)
    acc += gc_ref[...].astype(F32) * jnp.dot(yc_ref[...], wc_ref[...], preferred_element_type=F32)
    o_ref[...] = acc.astype(o_ref.dtype)


def _merge(ya, yb, yc, wa, wb, wc, gates):
    s = ya.shape[0]
    d = wa.shape[1]
    tn = MM_COLS
    nb = d // tn
    lhs = lambda width: pl.BlockSpec((MM_ROWS, width), lambda i, j: (i, 0))
    rhs = lambda width: pl.BlockSpec((width, tn), lambda i, j: (0, j))
    gate = lambda b: pl.BlockSpec((MM_ROWS, tn), lambda i, j: (i, b * nb + j))
    return pl.pallas_call(
        _merge_kernel,
        out_shape=jax.ShapeDtypeStruct((s, d), BF16),
        grid=(s // MM_ROWS, nb),
        in_specs=[lhs(ya.shape[1]), lhs(yb.shape[1]), lhs(yc.shape[1]),
                  rhs(wa.shape[0]), rhs(wb.shape[0]), rhs(wc.shape[0]),
                  gate(0), gate(1), gate(2)],
        out_specs=pl.BlockSpec((MM_ROWS, tn), lambda i, j: (i, j)),
        compiler_params=_cparams("parallel", "parallel"),
        name="merge",
    )(ya, yb, yc, wa, wb, wc, gates, gates, gates)


def _first_argmax_rows(v, iota, sentinel):
    mx = jnp.max(v, axis=0, keepdims=True)
    ix = jnp.min(jnp.where(v == mx, iota, sentinel), axis=0, keepdims=True)
    return mx, ix


def _router_kernel(h_ref, rwt_ref, bias_ref, upper_ref, idx_ref, w_ref, rank_ref, count_ref):
    @pl.when(pl.program_id(0) == 0)
    def _():
        count_ref[...] = jnp.zeros_like(count_ref)

    logits = lax.dot_general(rwt_ref[...], h_ref[...], (((1,), (1,)), ((), ())),
                             preferred_element_type=F32)
    scores = jax.nn.sigmoid(logits)
    biased = scores + bias_ref[...]
    n = logits.shape[1]
    sub = lax.broadcasted_iota(I32, (EXPERTS_PER_GROUP, n), 0)
    group_rows = []
    for g in range(N_EXPERT_GROUPS):
        blk = biased[g * EXPERTS_PER_GROUP:(g + 1) * EXPERTS_PER_GROUP, :]
        top1, i1 = _first_argmax_rows(blk, sub, EXPERTS_PER_GROUP)
        top2 = jnp.max(jnp.where(sub == i1, -jnp.inf, blk), axis=0, keepdims=True)
        group_rows.append(top1 + top2)
    gscore = jnp.concatenate(group_rows, axis=0)
    gi = lax.broadcasted_iota(I32, (N_EXPERT_GROUPS, n), 0)
    gsel = jnp.zeros((N_EXPERT_GROUPS, n), F32)
    for _ in range(TOPK_GROUPS):
        _, ix = _first_argmax_rows(gscore, gi, N_EXPERT_GROUPS)
        hit = gi == ix
        gsel = jnp.where(hit, 1.0, gsel)
        gscore = jnp.where(hit, -jnp.inf, gscore)
    emask = jnp.concatenate(
        [jnp.broadcast_to(gsel[g:g + 1, :], (EXPERTS_PER_GROUP, n)) for g in range(N_EXPERT_GROUPS)],
        axis=0)
    masked = jnp.where(emask > 0.0, biased, -jnp.inf)
    ei = lax.broadcasted_iota(I32, (N_EXPERTS, n), 0)
    idx_rows, w_rows, hits = [], [], []
    for _ in range(TOP_K):
        _, ix = _first_argmax_rows(masked, ei, N_EXPERTS)
        hit = ei == ix
        w_rows.append(jnp.sum(jnp.where(hit, scores, 0.0), axis=0, keepdims=True))
        idx_rows.append(ix)
        hits.append(hit)
        masked = jnp.where(hit, -jnp.inf, masked)
    total = w_rows[0]
    for wk in w_rows[1:]:
        total = total + wk
    pad = SUBLANES_V7X - TOP_K
    idx_ref[...] = jnp.concatenate(idx_rows + [jnp.zeros((pad, n), I32)], axis=0)
    w_ref[...] = jnp.concatenate([wk / total * ROUTED_SCALE for wk in w_rows]
                                 + [jnp.zeros((pad, n), F32)], axis=0)

    chosen = jnp.zeros((N_EXPERTS, n), F32)
    for hit in hits:
        chosen = jnp.where(hit, 1.0, chosen)
    before = jnp.dot(chosen.astype(BF16), upper_ref[...], preferred_element_type=F32)
    before = before + count_ref[:, 0:1]
    rank_rows = [jnp.sum(jnp.where(hit, before, 0.0), axis=0, keepdims=True) for hit in hits]
    rank_ref[...] = jnp.concatenate(rank_rows + [jnp.zeros((pad, n), F32)], axis=0).astype(I32)
    count_ref[...] = count_ref[...] + jnp.sum(chosen, axis=1, keepdims=True)


def _router(h_b, router_w, router_bias):
    t, d = h_b.shape
    rows = ROUTER_ROWS
    out = pl.BlockSpec((SUBLANES_V7X, rows), lambda i: (0, i))
    upper = (jnp.arange(rows)[:, None] < jnp.arange(rows)[None, :]).astype(BF16)
    idx, w, rank, count = pl.pallas_call(
        _router_kernel,
        out_shape=[jax.ShapeDtypeStruct((SUBLANES_V7X, t), I32),
                   jax.ShapeDtypeStruct((SUBLANES_V7X, t), F32),
                   jax.ShapeDtypeStruct((SUBLANES_V7X, t), I32),
                   jax.ShapeDtypeStruct((N_EXPERTS, LANES_V7X), F32)],
        grid=(t // rows,),
        in_specs=[pl.BlockSpec((rows, d), lambda i: (i, 0)),
                  pl.BlockSpec((N_EXPERTS, d), lambda i: (0, 0)),
                  pl.BlockSpec((N_EXPERTS, 1), lambda i: (0, 0)),
                  pl.BlockSpec((rows, rows), lambda i: (0, 0))],
        out_specs=[out, out, out, pl.BlockSpec((N_EXPERTS, LANES_V7X), lambda i: (0, 0))],
        compiler_params=_cparams("arbitrary"),
        name="router",
    )(h_b, router_w.T.astype(BF16), router_bias.reshape(N_EXPERTS, 1), upper)
    return idx[:TOP_K].T, w.T, rank[:TOP_K].T, count[:, 0].astype(I32)


def _route_tiles(counts, n_pairs):
    tm = MOE_ROWS
    n_tiles = n_pairs // tm + N_EXPERTS
    padded = ((counts + tm - 1) // tm) * tm
    off_end = jnp.cumsum(padded)
    off = off_end - padded
    n_used = off_end[-1] // tm
    tiles = jnp.arange(n_tiles, dtype=I32)
    tile_valid = tiles < n_used
    tile_expert = jnp.minimum(jnp.sum((off_end[None, :] <= (tiles * tm)[:, None]).astype(I32), axis=1),
                              N_EXPERTS - 1).astype(I32)
    tile_expert = jnp.where(tile_valid, tile_expert, tile_expert[n_used - 1])
    tile_first = jnp.concatenate([jnp.ones((1,), I32),
                                  (tile_expert[1:] != tile_expert[:-1]).astype(I32)])
    return (off.astype(I32), (off + counts).astype(I32), off_end.astype(I32), tile_expert,
            tile_first, tile_valid.astype(I32), jnp.maximum(n_used - 1, 0).astype(I32).reshape(1))


def _pack_rows(v):
    half = v.shape[1] // 2
    return (_bf16_bits(v[:, :half]) >> 16) | _bf16_bits(v[:, half:])


def _unpack_rows(u):
    lo = pltpu.bitcast(u << 16, F32)
    hi = pltpu.bitcast(u & jnp.uint32(0xFFFF0000), F32)
    return jnp.concatenate([lo, hi], axis=1)


def _dispatch_kernel(fill_lo_ref, fill_hi_ref, last_ref, pos_ref, hp_ref, hb_ref, wg_ref, wu_ref,
                     wd_ref, xs_hbm, shared_ref, zero_ref, sem):
    rows = hp_ref.shape[0]
    tile_rows = zero_ref.shape[0]
    n_tiles = xs_hbm.shape[0] // tile_rows

    def row_to_slot(src_row_ref, slot):
        return pltpu.make_async_copy(src_row_ref, xs_hbm.at[pl.ds(slot, 1), :], sem)

    def zeros_to_tile(tile):
        start = pl.multiple_of(tile * tile_rows, tile_rows)
        return pltpu.make_async_copy(zero_ref, xs_hbm.at[pl.ds(start, tile_rows), :], sem)

    @pl.when(pl.program_id(0) == 0)
    def _():
        zero_ref[...] = jnp.zeros_like(zero_ref)
        zero_row = zero_ref.at[pl.ds(0, 1), :]

        def fill_expert(e, carry):
            lax.fori_loop(fill_lo_ref[e], fill_hi_ref[e],
                          lambda s, c: (row_to_slot(zero_row, s).start(), c)[1], 0)
            return carry

        def drain_expert(e, carry):
            lax.fori_loop(fill_lo_ref[e], fill_hi_ref[e],
                          lambda s, c: (row_to_slot(zero_row, s).wait(), c)[1], 0)
            return carry

        lax.fori_loop(0, N_EXPERTS, fill_expert, 0)
        lax.fori_loop(last_ref[0] + 1, n_tiles, lambda tl, c: (zeros_to_tile(tl).start(), c)[1], 0)
        lax.fori_loop(0, N_EXPERTS, drain_expert, 0)
        lax.fori_loop(last_ref[0] + 1, n_tiles, lambda tl, c: (zeros_to_tile(tl).wait(), c)[1], 0)

    def issue(r, carry):
        src = hp_ref.at[pl.ds(r, 1), :]
        for k in range(TOP_K):
            row_to_slot(src, pos_ref[0, r * TOP_K + k]).start()
        return carry

    def drain(r, carry):
        for k in range(TOP_K):
            row_to_slot(hp_ref.at[pl.ds(0, 1), :], 0).wait()
        return carry

    lax.fori_loop(0, rows, issue, 0)
    h = hb_ref[...]
    act = _silu(jnp.dot(h, wg_ref[...], preferred_element_type=F32)) * jnp.dot(
        h, wu_ref[...], preferred_element_type=F32)
    shared_ref[...] = jnp.dot(act.astype(BF16), wd_ref[...], preferred_element_type=F32)
    lax.fori_loop(0, rows, drain, 0)


def _dispatch(h_packed, h_b, pos_tiles, fill_lo, fill_hi, last_tile, ws_gate, ws_up, ws_down, n_slots):
    t, half = h_packed.shape
    d = h_b.shape[1]
    f = ws_gate.shape[1]
    tm = DISPATCH_ROWS
    const = lambda shape: pl.BlockSpec(shape, lambda i, lo, hi, last: (0,) * len(shape))
    grid_spec = pltpu.PrefetchScalarGridSpec(
        num_scalar_prefetch=3,
        grid=(t // tm,),
        in_specs=[pl.BlockSpec((None, 1, tm * TOP_K), lambda i, lo, hi, last: (i, 0, 0),
                               memory_space=pltpu.SMEM),
                  pl.BlockSpec((tm, half), lambda i, lo, hi, last: (i, 0)),
                  pl.BlockSpec((tm, d), lambda i, lo, hi, last: (i, 0)),
                  const((d, f)), const((d, f)), const((f, d))],
        out_specs=[pl.BlockSpec(memory_space=pl.ANY),
                   pl.BlockSpec((tm, d), lambda i, lo, hi, last: (i, 0))],
        scratch_shapes=[pltpu.VMEM((MOE_ROWS, half), U32), pltpu.SemaphoreType.DMA(())],
    )
    return pl.pallas_call(
        _dispatch_kernel,
        out_shape=[jax.ShapeDtypeStruct((n_slots, half), U32), jax.ShapeDtypeStruct((t, d), F32)],
        grid_spec=grid_spec,
        compiler_params=_cparams("arbitrary"),
        name="moe_dispatch",
    )(fill_lo, fill_hi, last_tile, pos_tiles, h_packed, h_b, ws_gate, ws_up, ws_down)


def _moe_group_kernel(te_ref, tf_ref, tv_ref, last_ref, x_ref, wg_ref, wu_ref, wd_ref, y_ref,
                      wg_b, wu_b, wd_b):
    t = pl.program_id(0)

    @pl.when(tf_ref[t] == 1)
    def _():
        wg_b[...] = wg_ref[...].astype(BF16)
        wu_b[...] = wu_ref[...].astype(BF16)
        wd_b[...] = wd_ref[...].astype(BF16)

    @pl.when(tv_ref[t] == 1)
    def _():
        x = _unpack_rows(x_ref[...]).astype(BF16)
        gate = jnp.dot(x, wg_b[...], preferred_element_type=F32)
        up = jnp.dot(x, wu_b[...], preferred_element_type=F32)
        act = _silu(gate) * up
        y_ref[...] = _pack_rows(jnp.dot(act.astype(BF16), wd_b[...], preferred_element_type=F32))

    @pl.when(tv_ref[t] == 0)
    def _():
        y_ref[...] = jnp.zeros_like(y_ref)


def _moe_grouped(x_sorted, tile_expert, tile_first, tile_valid, last_tile, w_gate, w_up, w_down, layer):
    p, half = x_sorted.shape
    d, f = w_gate.shape[2], w_gate.shape[3]
    tm = MOE_ROWS
    grid_spec = pltpu.PrefetchScalarGridSpec(
        num_scalar_prefetch=4,
        grid=(p // tm,),
        in_specs=[pl.BlockSpec((tm, half), lambda t, te, tf, tv, last: (jnp.minimum(t, last[0]), 0)),
                  pl.BlockSpec((None, None, d, f), lambda t, te, tf, tv, last: (layer, te[t], 0, 0)),
                  pl.BlockSpec((None, None, d, f), lambda t, te, tf, tv, last: (layer, te[t], 0, 0)),
                  pl.BlockSpec((None, None, f, d), lambda t, te, tf, tv, last: (layer, te[t], 0, 0))],
        out_specs=pl.BlockSpec((tm, half), lambda t, te, tf, tv, last: (t, 0)),
        scratch_shapes=[pltpu.VMEM((d, f), BF16), pltpu.VMEM((d, f), BF16), pltpu.VMEM((f, d), BF16)],
    )
    return pl.pallas_call(
        _moe_group_kernel,
        out_shape=jax.ShapeDtypeStruct((p, half), U32),
        grid_spec=grid_spec,
        compiler_params=_cparams("arbitrary"),
        name="moe_grouped",
    )(tile_expert, tile_first, tile_valid, last_tile, x_sorted, w_gate, w_up, w_down)


def _combine_kernel(pos_ref, y_hbm, w_ref, shared_ref, x_ref, gate_ref, o_ref, buf_ref, sem):
    rows = x_ref.shape[0]

    def slot_to_row(slot, k, r):
        return pltpu.make_async_copy(y_hbm.at[pl.ds(slot, 1), :], buf_ref.at[k, pl.ds(r, 1), :], sem)

    def issue(r, carry):
        for k in range(TOP_K):
            slot_to_row(pos_ref[0, r * TOP_K + k], k, r).start()
        return carry

    def drain(r, carry):
        for k in range(TOP_K):
            slot_to_row(0, k, r).wait()
        return carry

    lax.fori_loop(0, rows, issue, 0)
    lax.fori_loop(0, rows, drain, 0)
    total = shared_ref[...]
    w = w_ref[...]
    for k in range(TOP_K):
        total = total + w[:, k:k + 1] * _unpack_rows(buf_ref[k])
    o_ref[...] = x_ref[...] + gate_ref[...] * total


def _combine(y_sorted, pos_tiles, wsel, shared, x, gate):
    t, d = x.shape
    half = y_sorted.shape[1]
    tm = DISPATCH_ROWS
    blk = pl.BlockSpec((tm, d), lambda i: (i, 0))
    return pl.pallas_call(
        _combine_kernel,
        out_shape=jax.ShapeDtypeStruct((t, d), F32),
        grid=(t // tm,),
        in_specs=[pl.BlockSpec((None, 1, tm * TOP_K), lambda i: (i, 0, 0), memory_space=pltpu.SMEM),
                  pl.BlockSpec(memory_space=pl.ANY),
                  pl.BlockSpec((tm, SUBLANES_V7X), lambda i: (i, 0)),
                  blk, blk, pl.BlockSpec((1, d), lambda i: (0, 0))],
        out_specs=blk,
        scratch_shapes=[pltpu.VMEM((TOP_K, tm, half), U32), pltpu.SemaphoreType.DMA(())],
        compiler_params=_cparams("arbitrary"),
        name="moe_combine",
    )(pos_tiles, y_sorted, wsel, shared, x, gate)


def _in_proj_weights(w_in):
    sizes = (DIFF_WIDTH, DIFF_WIDTH, DIFF_WIDTH, SSD_D_INNER, SSD_CONV_DIM, SSD_HEADS,
             FOX_WIDTH, FOX_WIDTH, FOX_WIDTH, FOX_HEADS, N_BRANCHES * D_MODEL)
    cuts = [0]
    for sz in sizes:
        cuts.append(cuts[-1] + sz)
    seg = lambda a, b: w_in[:, cuts[a]:cuts[b]].astype(BF16)
    small = jnp.zeros((w_in.shape[0], SMALL_COLS), F32)
    small = small.at[:, DT_LANE0:DT_LANE0 + SSD_HEADS].set(w_in[:, cuts[5]:cuts[6]])
    small = small.at[:, FF_LANE0:FF_LANE0 + FOX_HEADS].set(w_in[:, cuts[9]:cuts[10]])
    return dict(diff_qk=seg(0, 2), diff_v=seg(2, 3), ssd_z=seg(3, 4), ssd_xbc=seg(4, 5),
                small=small.astype(BF16), fox=seg(6, 9), gates=seg(10, 11))


def _mixer(x, mod, cos, sin_signed, layer_idx, p):
    s = x.shape[0]
    sh1, sc1, g1 = mod[0], mod[1], mod[2]
    (h,) = _norm_mod(x, p["norm_mix_w"], sc1, sh1, (BF16,))
    w = _in_proj_weights(p["w_in"])
    qscale = HEAD_DIM ** -0.5 * LOG2E

    ones = jnp.ones((1, DIFF_WIDTH), F32)
    diff_qk = _matmul(h, w["diff_qk"], BF16, "rope",
                      (cos, sin_signed, jnp.concatenate([ones * qscale, ones], axis=1)), name="proj_diff_qk")
    diff_v = _matmul(h, w["diff_v"], BF16, name="proj_diff_v")
    ssd_z = _matmul(h, w["ssd_z"], F32, name="proj_ssd_z")
    ssd_xbc = _matmul(h, w["ssd_xbc"], F32, name="proj_ssd_xbc")
    small = _matmul(h, w["small"], F32, name="proj_small")
    fox_scale = jnp.concatenate([jnp.full((1, FOX_WIDTH), qscale, F32),
                                 jnp.ones((1, 2 * FOX_WIDTH), F32)], axis=1)
    fox = _matmul(h, w["fox"], BF16, "colscale", (fox_scale,), name="proj_fox")
    gates = _matmul(h, w["gates"], BF16, "sigmoid", name="proj_gates")

    o_diff = _flash(diff_qk, diff_qk, diff_v, n_heads=2 * DIFF_HEADS, q_col0=0, k_col0=DIFF_WIDTH,
                    v_col0=0, dv=DIFF_V_DIM, v_rep=2, mask_shift=int(math.log2(CHUNK)),
                    key_bias=None, out_dtype=F32, name="diff_attn")
    ya = _diff_combine(o_diff, p["diff_lambda"], p["diff_subln_w"], layer_idx)

    yb, key_bias = _ssd(ssd_z, ssd_xbc, small, p["ssd_conv_w"], p["ssd_conv_b"], p["ssd_dt_bias"],
                        p["ssd_a_log"], p["fox_f_bias"], p["ssd_d"], p["ssd_norm_w"])

    yc = _flash(fox, fox, fox, n_heads=FOX_HEADS, q_col0=0, k_col0=FOX_WIDTH, v_col0=2 * FOX_WIDTH,
                dv=HEAD_DIM, v_rep=1, mask_shift=0, key_bias=key_bias, out_dtype=BF16,
                name="fox_attn")

    merged = _merge(ya, yb, yc, p["w_br_diff"].astype(BF16), p["w_br_ssd"].astype(BF16),
                    p["w_br_fox"].astype(BF16), gates)
    return _matmul(merged, p["w_out"].astype(BF16), F32, "residual", (x, g1), name="out_proj")


def _moe(x, mod, p, layer, moe_w_gate, moe_w_up, moe_w_down):
    t = x.shape[0]
    sh2, sc2, g2 = mod[3], mod[4], mod[5]
    h_b, h_packed = _norm_mod(x, p["norm_ffn_w"], sc2, sh2, (BF16, U32))
    eidx, wsel, rank, counts = _router(h_b, p["router_w"], p["router_bias"])
    n_pairs = t * TOP_K
    off, fill_lo, fill_hi, tile_expert, tile_first, tile_valid, last_tile = _route_tiles(counts, n_pairs)
    pos_tiles = (off[eidx] + rank).reshape(t // DISPATCH_ROWS, 1, DISPATCH_ROWS * TOP_K)
    n_slots = n_pairs + N_EXPERTS * MOE_ROWS
    x_sorted, shared = _dispatch(h_packed, h_b, pos_tiles, fill_lo, fill_hi, last_tile,
                                 p["shared_w_gate"].astype(BF16), p["shared_w_up"].astype(BF16),
                                 p["shared_w_down"].astype(BF16), n_slots)
    y_sorted = _moe_grouped(x_sorted, tile_expert, tile_first, tile_valid, last_tile,
                            moe_w_gate, moe_w_up, moe_w_down, layer)
    return _combine(y_sorted, pos_tiles, wsel, shared, x, g2)


_LAYER_PARAMS = ("norm_mix_w", "norm_ffn_w", "w_in", "diff_lambda", "diff_subln_w", "ssd_conv_w",
                 "ssd_conv_b", "ssd_dt_bias", "ssd_a_log", "ssd_d", "ssd_norm_w", "fox_f_bias",
                 "w_br_diff", "w_br_ssd", "w_br_fox", "w_out", "router_w", "router_bias",
                 "shared_w_gate", "shared_w_up", "shared_w_down")


def kernel(x, c, positions, ada_w, ada_b, norm_mix_w, norm_ffn_w, w_in, diff_lambda, diff_subln_w, ssd_conv_w, ssd_conv_b, ssd_dt_bias, ssd_a_log, ssd_d, ssd_norm_w, fox_f_bias, w_br_diff, w_br_ssd, w_br_fox, w_out, router_w, router_bias, moe_w_gate, moe_w_up, moe_w_down, shared_w_gate, shared_w_up, shared_w_down, final_norm_w):
    stacked = dict(norm_mix_w=norm_mix_w, norm_ffn_w=norm_ffn_w, w_in=w_in, diff_lambda=diff_lambda,
                   diff_subln_w=diff_subln_w, ssd_conv_w=ssd_conv_w, ssd_conv_b=ssd_conv_b,
                   ssd_dt_bias=ssd_dt_bias, ssd_a_log=ssd_a_log, ssd_d=ssd_d, ssd_norm_w=ssd_norm_w,
                   fox_f_bias=fox_f_bias, w_br_diff=w_br_diff, w_br_ssd=w_br_ssd, w_br_fox=w_br_fox,
                   w_out=w_out, router_w=router_w, router_bias=router_bias,
                   shared_w_gate=shared_w_gate, shared_w_up=shared_w_up, shared_w_down=shared_w_down)
    batch, seq, d = x.shape
    assert batch == 1 and d == D_MODEL and seq % MM_ROWS == 0
    xs = x.reshape(seq, d)
    mods = _adaln(c, ada_w, ada_b)
    cos, sin_signed = _rope_tables(positions)
    for l in range(DEPTH):
        p = {name: stacked[name][l] for name in _LAYER_PARAMS}
        mod = [mods[l, :, i * d:(i + 1) * d] for i in range(6)]
        xs = _mixer(xs, mod, cos, sin_signed, l, p)
        xs = _moe(xs, mod, p, l, moe_w_gate, moe_w_up, moe_w_down)
    zero = jnp.zeros((1, d), F32)
    (out,) = _norm_mod(xs, final_norm_w, zero, zero, (F32,))
    return out.reshape(batch, seq, d)
```

```python
import functools
import math

import jax
import jax.numpy as jnp
from jax import lax
from jax.experimental import pallas as pl
from jax.experimental.pallas import tpu as pltpu

F32 = jnp.float32
BF16 = jnp.bfloat16
I32 = jnp.int32
U32 = jnp.uint32

D_MODEL = 2048
DEPTH = 2
CHUNK = 64
ROPE_THETA = 10000.0
NORM_EPS = 1e-6
DIFF_HEADS = 4
HEAD_DIM = 128
DIFF_V_DIM = 2 * HEAD_DIM
DIFF_WIDTH = DIFF_HEADS * DIFF_V_DIM
SSD_D_INNER = D_MODEL
SSD_HEAD_DIM = 64
SSD_HEADS = SSD_D_INNER // SSD_HEAD_DIM
SSD_GROUPS = 4
SSD_STATE = 128
SSD_CONV = 4
SSD_CONV_DIM = SSD_D_INNER + 2 * SSD_GROUPS * SSD_STATE
SSD_GROUP_WIDTH = SSD_D_INNER // SSD_GROUPS
FOX_HEADS = 8
FOX_WIDTH = FOX_HEADS * HEAD_DIM
N_BRANCHES = 3
N_EXPERTS = 64
TOP_K = 6
N_EXPERT_GROUPS = 8
EXPERTS_PER_GROUP = N_EXPERTS // N_EXPERT_GROUPS
TOPK_GROUPS = 4
D_EXPERT = 512
D_SHARED = 512
ROUTED_SCALE = 2.5
LOG2E = math.log2(math.e)

LANES_V7X = 128
SUBLANES_V7X = 8
VMEM_BYTES_V7X = 64 * 1024 * 1024
VMEM_LIMIT_BYTES = VMEM_BYTES_V7X - 8 * 1024 * 1024

NORM_ROWS = 512
MM_ROWS = 1024
MM_COLS = 1024
MERGE_COLS = 512
ATTN_BLOCK = 1024
SSD_BLOCK = 128
ROUTER_ROWS = 512
MOE_ROWS = 512
DISPATCH_ROWS = 256
ADALN_COLS = 1024
SMALL_COLS = LANES_V7X
DT_LANE0 = 0
FF_LANE0 = SSD_HEADS


def _cparams(*semantics):
    return pltpu.CompilerParams(dimension_semantics=semantics, vmem_limit_bytes=VMEM_LIMIT_BYTES)


def _silu(v):
    return v * jax.nn.sigmoid(v)


def _softplus(v):
    return jnp.maximum(v, 0.0) + jnp.log1p(jnp.exp(-jnp.abs(v)))


def _adaln_kernel(c_ref, w_ref, b_ref, o_ref):
    cond = _silu(c_ref[...]).astype(BF16)
    o_ref[...] = jnp.dot(cond, w_ref[...].astype(BF16), preferred_element_type=F32) + b_ref[...]


def _adaln(c, ada_w, ada_b):
    n_layers, d, n = ada_w.shape
    c8 = jnp.broadcast_to(c.reshape(1, d), (SUBLANES_V7X, d))
    out = pl.pallas_call(
        _adaln_kernel,
        out_shape=jax.ShapeDtypeStruct((n_layers, SUBLANES_V7X, n), F32),
        grid=(n_layers, n // ADALN_COLS),
        in_specs=[pl.BlockSpec((SUBLANES_V7X, d), lambda l, j: (0, 0)),
                  pl.BlockSpec((None, d, ADALN_COLS), lambda l, j: (l, 0, j)),
                  pl.BlockSpec((None, 1, ADALN_COLS), lambda l, j: (l, 0, j))],
        out_specs=pl.BlockSpec((None, SUBLANES_V7X, ADALN_COLS), lambda l, j: (l, 0, j)),
        compiler_params=_cparams("parallel", "parallel"),
        name="adaln",
    )(c8, ada_w, ada_b.reshape(n_layers, 1, n))
    return out[:, 0:1, :]


def _norm_mod_kernel(x_ref, w_ref, sc_ref, sh_ref, *o_refs):
    x = x_ref[...]
    y = x * lax.rsqrt(jnp.mean(x * x, axis=-1, keepdims=True) + NORM_EPS)
    h = (y * w_ref[...]) * (1.0 + sc_ref[...]) + sh_ref[...]
    for o_ref in o_refs:
        o_ref[...] = _pack_rows(h) if o_ref.dtype == U32 else h.astype(o_ref.dtype)


def _norm_mod(x, w, scale, shift, out_dtypes):
    s, d = x.shape
    row = pl.BlockSpec((1, d), lambda i: (0, 0))
    blk = pl.BlockSpec((NORM_ROWS, d), lambda i: (i, 0))
    width = lambda dt: d // 2 if dt == U32 else d
    outs = pl.pallas_call(
        _norm_mod_kernel,
        out_shape=[jax.ShapeDtypeStruct((s, width(dt)), dt) for dt in out_dtypes],
        grid=(s // NORM_ROWS,),
        in_specs=[blk, row, row, row],
        out_specs=[pl.BlockSpec((NORM_ROWS, width(dt)), lambda i: (i, 0)) for dt in out_dtypes],
        compiler_params=_cparams("parallel"),
        name="norm_mod",
    )(x, w.reshape(1, d), scale.reshape(1, d), shift.reshape(1, d))
    return outs


def _rope_rotate(v, cos, sin_signed):
    return v * cos + pltpu.roll(v, HEAD_DIM // 2, 1) * sin_signed


def _matmul_kernel(*refs, epilogue):
    a_ref, b_ref = refs[0], refs[1]
    o_ref = refs[-1]
    acc = jnp.dot(a_ref[...], b_ref[...], preferred_element_type=F32)
    if epilogue == "sigmoid":
        acc = jax.nn.sigmoid(acc)
    elif epilogue == "rope":
        cos_ref, sin_ref, scale_ref = refs[2], refs[3], refs[4]
        cos, sin_signed = cos_ref[...], sin_ref[...]
        parts = [_rope_rotate(acc[:, g * HEAD_DIM:(g + 1) * HEAD_DIM], cos, sin_signed)
                 for g in range(acc.shape[1] // HEAD_DIM)]
        acc = jnp.concatenate(parts, axis=1) * scale_ref[...]
    elif epilogue == "colscale":
        acc = acc * refs[2][...]
    elif epilogue == "residual":
        res_ref, gate_ref = refs[2], refs[3]
        acc = res_ref[...] + gate_ref[...] * acc
    o_ref[...] = acc.astype(o_ref.dtype)


def _matmul(a, b, out_dtype, epilogue="none", extra=(), cols=MM_COLS, name="matmul"):
    m, k = a.shape
    n = b.shape[1]
    tn = min(cols, n)
    in_specs = [pl.BlockSpec((MM_ROWS, k), lambda i, j: (i, 0)),
                pl.BlockSpec((k, tn), lambda i, j: (0, j))]
    row_tile = pl.BlockSpec((1, tn), lambda i, j: (0, j))
    if epilogue == "rope":
        tab = pl.BlockSpec((MM_ROWS, HEAD_DIM), lambda i, j: (i, 0))
        in_specs += [tab, tab, row_tile]
    elif epilogue == "colscale":
        in_specs += [row_tile]
    elif epilogue == "residual":
        in_specs += [pl.BlockSpec((MM_ROWS, tn), lambda i, j: (i, j)), row_tile]
    return pl.pallas_call(
        functools.partial(_matmul_kernel, epilogue=epilogue),
        out_shape=jax.ShapeDtypeStruct((m, n), out_dtype),
        grid=(m // MM_ROWS, n // tn),
        in_specs=in_specs,
        out_specs=pl.BlockSpec((MM_ROWS, tn), lambda i, j: (i, j)),
        compiler_params=_cparams("parallel", "parallel"),
        name=name,
    )(a, b, *extra)


def _rope_table_kernel(pos_ref, freq_ref, sign_ref, cos_ref, sin_ref):
    ang = pos_ref[...].astype(F32) * freq_ref[...]
    cos_ref[...] = jnp.cos(ang)
    sin_ref[...] = jnp.sin(ang) * sign_ref[...]


def _rope_tables(positions):
    s = positions.shape[-1]
    half = HEAD_DIM // 2
    inv_freq = 1.0 / (ROPE_THETA ** (jnp.arange(half, dtype=F32) * 2.0 / HEAD_DIM))
    freq = jnp.concatenate([inv_freq, inv_freq]).reshape(1, HEAD_DIM)
    sign = jnp.concatenate([-jnp.ones((half,), F32), jnp.ones((half,), F32)]).reshape(1, HEAD_DIM)
    row = pl.BlockSpec((1, HEAD_DIM), lambda i: (0, 0))
    tab = pl.BlockSpec((NORM_ROWS, HEAD_DIM), lambda i: (i, 0))
    return pl.pallas_call(
        _rope_table_kernel,
        out_shape=[jax.ShapeDtypeStruct((s, HEAD_DIM), F32)] * 2,
        grid=(s // NORM_ROWS,),
        in_specs=[pl.BlockSpec((NORM_ROWS, 1), lambda i: (i, 0)), row, row],
        out_specs=[tab, tab],
        compiler_params=_cparams("parallel"),
        name="rope_tables",
    )(positions.reshape(s, 1), freq, sign)


BIAS_PIECES = 3


def _flash_kernel(*refs, block, mask_shift, has_bias):
    if has_bias:
        q_ref, k_ref, vt_ref, kb_ref, o_ref, s_scr, m_scr, l_scr, acc_scr = refs
    else:
        q_ref, k_ref, vt_ref, o_ref, s_scr, m_scr, l_scr, acc_scr = refs
        kb_ref = None
    i = pl.program_id(1)
    q = q_ref[...]
    if has_bias:
        lane = lax.broadcasted_iota(I32, (block, HEAD_DIM), 1)
        q = jnp.concatenate([q, jnp.where(lane < BIAS_PIECES, 1.0, 0.0).astype(BF16)], axis=1)

    def scores(j, slot):
        start = pl.multiple_of(j * block, block)
        k = k_ref[pl.ds(start, block), :]
        if has_bias:
            k = jnp.concatenate([k, kb_ref[pl.ds(start, block), :]], axis=1)
        s_scr[slot] = lax.dot_general(k, q, (((1,), (1,)), ((), ())),
                                      preferred_element_type=F32)

    def update(j, slot, masked):
        s = s_scr[slot]
        if masked:
            key = lax.broadcasted_iota(I32, (block, block), 0) >> mask_shift
            qry = lax.broadcasted_iota(I32, (block, block), 1) >> mask_shift
            s = jnp.where(key <= qry, s, -jnp.inf)
        m = m_scr[...]
        m_new = jnp.maximum(m, jnp.max(s, axis=0, keepdims=True))
        alpha = jnp.exp2(m - m_new)
        p = jnp.exp2(s - m_new)
        l_scr[...] = alpha * l_scr[...] + jnp.sum(p, axis=0, keepdims=True)
        acc_scr[...] = alpha * acc_scr[...] + jnp.dot(vt_ref[j], p.astype(BF16),
                                                      preferred_element_type=F32)
        m_scr[...] = m_new

    m_scr[...] = jnp.full(m_scr.shape, -jnp.inf, F32)
    l_scr[...] = jnp.zeros(l_scr.shape, F32)
    acc_scr[...] = jnp.zeros(acc_scr.shape, F32)

    scores(0, 0)

    def pair(jj, carry):
        j = 2 * jj
        scores(j + 1, 1)
        update(j, 0, False)
        scores(j + 2, 0)
        update(j + 1, 1, False)
        return carry

    lax.fori_loop(0, i // 2, pair, 0)

    @pl.when(i % 2 == 0)
    def _():
        update(i, 0, True)

    @pl.when(i % 2 == 1)
    def _():
        scores(i, 1)
        update(i - 1, 0, False)
        update(i, 1, True)

    o_ref[...] = (acc_scr[...] / l_scr[...]).T.astype(o_ref.dtype)


def _flash(q_arr, k_arr, v_arr, *, n_heads, q_col0, k_col0, v_col0, dv, v_rep, mask_shift,
           key_bias, out_dtype, name):
    s = q_arr.shape[0]
    blk = ATTN_BLOCK
    nkb = s // blk
    qb, kb0 = q_col0 // HEAD_DIM, k_col0 // HEAD_DIM
    n_vheads = n_heads // v_rep
    v_t = v_arr[:, v_col0:v_col0 + n_vheads * dv].reshape(nkb, blk, n_vheads, dv).transpose(2, 0, 3, 1)
    resident = dict(pipeline_mode=pl.Buffered(1))
    in_specs = [pl.BlockSpec((blk, HEAD_DIM), lambda h, i: (i, qb + h)),
                pl.BlockSpec((s, HEAD_DIM), lambda h, i: (0, kb0 + h), **resident),
                pl.BlockSpec((None, nkb, dv, blk), lambda h, i: (h // v_rep, 0, 0, 0), **resident)]
    args = [q_arr, k_arr, v_t]
    if key_bias is not None:
        in_specs.append(pl.BlockSpec((None, s, HEAD_DIM), lambda h, i: (h, 0, 0), **resident))
        args.append(key_bias)
    return pl.pallas_call(
        functools.partial(_flash_kernel, block=blk, mask_shift=mask_shift,
                          has_bias=key_bias is not None),
        out_shape=jax.ShapeDtypeStruct((s, n_heads * dv), out_dtype),
        grid=(n_heads, nkb),
        in_specs=in_specs,
        out_specs=pl.BlockSpec((blk, dv), lambda h, i: (i, h)),
        scratch_shapes=[pltpu.VMEM((2, blk, blk), F32), pltpu.VMEM((1, blk), F32),
                        pltpu.VMEM((1, blk), F32), pltpu.VMEM((dv, blk), F32)],
        compiler_params=_cparams("parallel", "arbitrary"),
        name=name,
    )(*args)


def _diff_combine_kernel(o_ref, lam_ref, w_ref, y_ref, *, lam_init):
    lp = lam_ref[...]
    s1 = jnp.sum(lp[0:1] * lp[1:2], axis=-1, keepdims=True)
    s2 = jnp.sum(lp[2:3] * lp[3:4], axis=-1, keepdims=True)
    lam = jnp.exp(s1) - jnp.exp(s2) + lam_init
    w = w_ref[...]
    for h in range(DIFF_HEADS):
        o1 = o_ref[:, (2 * h) * DIFF_V_DIM:(2 * h + 1) * DIFF_V_DIM]
        o2 = o_ref[:, (2 * h + 1) * DIFF_V_DIM:(2 * h + 2) * DIFF_V_DIM]
        d = o1 - lam * o2
        y = d * lax.rsqrt(jnp.mean(d * d, axis=-1, keepdims=True) + NORM_EPS) * w
        y_ref[:, h * DIFF_V_DIM:(h + 1) * DIFF_V_DIM] = (y * (1.0 - lam_init)).astype(y_ref.dtype)


def _diff_combine(o, lam_params, subln_w, layer_idx):
    s = o.shape[0]
    lam_init = 0.8 - 0.6 * math.exp(-0.3 * layer_idx)
    return pl.pallas_call(
        functools.partial(_diff_combine_kernel, lam_init=lam_init),
        out_shape=jax.ShapeDtypeStruct((s, DIFF_WIDTH), BF16),
        grid=(s // NORM_ROWS,),
        in_specs=[pl.BlockSpec((NORM_ROWS, 2 * DIFF_WIDTH), lambda i: (i, 0)),
                  pl.BlockSpec((4, HEAD_DIM), lambda i: (0, 0)),
                  pl.BlockSpec((1, DIFF_V_DIM), lambda i: (0, 0))],
        out_specs=pl.BlockSpec((NORM_ROWS, DIFF_WIDTH), lambda i: (i, 0)),
        compiler_params=_cparams("parallel"),
        name="diff_combine",
    )(o, lam_params, subln_w.reshape(1, DIFF_V_DIM))


def _cumsum_rows(v):
    n = v.shape[0]
    row = lax.broadcasted_iota(I32, (n, 1), 0)
    shift = 1
    while shift < n:
        v = v + jnp.where(row >= shift, pltpu.roll(v, shift, 0), 0.0)
        shift *= 2
    return v


def _bf16_bits(v):
    u = pltpu.bitcast(v, U32)
    return (u + jnp.uint32(0x7FFF) + ((u >> 16) & jnp.uint32(1))) & jnp.uint32(0xFFFF0000)


def _split_bf16x3(v):
    hi = pltpu.bitcast(_bf16_bits(v), F32)
    r1 = v - hi
    mid = pltpu.bitcast(_bf16_bits(r1), F32)
    lo = r1 - mid
    return hi.astype(BF16), mid.astype(BF16), lo.astype(BF16)


def _expand_heads(v, e_ref):
    hi, mid, lo = _split_bf16x3(v)
    e = e_ref[...]
    return (jnp.dot(hi, e, preferred_element_type=F32) + jnp.dot(mid, e, preferred_element_type=F32)
            + jnp.dot(lo, e, preferred_element_type=F32))


def _ssd_kernel(z_ref, xbc_ref, small_ref, convw_ref, convb_ref, dtb_ref, alog_ref, fb_ref,
                dskip_ref, normw_ref, e64_ref, e128_ref, y_ref, kb_ref,
                prev_ref, state_ref, fcarry_ref):
    q = SSD_BLOCK
    gw = SSD_GROUP_WIDTH

    @pl.when(pl.program_id(0) == 0)
    def _():
        prev_ref[...] = jnp.zeros_like(prev_ref)
        state_ref[...] = jnp.zeros_like(state_ref)
        fcarry_ref[...] = jnp.zeros_like(fcarry_ref)

    cur = xbc_ref[...]
    prev = prev_ref[...]
    row = lax.broadcasted_iota(I32, (q, 1), 0)
    conv = cur * convw_ref[SSD_CONV - 1:SSD_CONV, :]
    for k in range(1, SSD_CONV):
        shifted = jnp.where(row < k, pltpu.roll(prev, k, 0), pltpu.roll(cur, k, 0))
        conv = conv + shifted * convw_ref[SSD_CONV - 1 - k:SSD_CONV - k, :]
    prev_ref[...] = cur
    xa = _silu(conv + convb_ref[...])
    xs = xa[:, :SSD_D_INNER]

    small = small_ref[...]
    lane = lax.broadcasted_iota(I32, (1, SMALL_COLS), 1)
    is_dt = (lane >= DT_LANE0) & (lane < DT_LANE0 + SSD_HEADS)
    is_ff = (lane >= FF_LANE0) & (lane < FF_LANE0 + FOX_HEADS)

    logf = jnp.where(is_ff, -_softplus(-(small + fb_ref[...])), 0.0)
    cum = _cumsum_rows(logf) + fcarry_ref[...]
    fcarry_ref[...] = cum[q - 1:q, :]
    hi, mid, lo = (v.astype(F32) for v in _split_bf16x3(cum * (-LOG2E)))
    lane_q = lax.broadcasted_iota(I32, (q, LANES_V7X), 1)
    for h in range(FOX_HEADS):
        col = FF_LANE0 + h
        piece = lambda v: jnp.broadcast_to(v[:, col:col + 1], (q, LANES_V7X))
        kb_ref[h] = jnp.where(lane_q == 0, piece(hi),
                              jnp.where(lane_q == 1, piece(mid),
                                        jnp.where(lane_q == 2, piece(lo), 0.0))).astype(BF16)

    dt = jnp.where(is_dt, _softplus(small + dtb_ref[...]), 0.0)
    a = dt * (-jnp.exp(alog_ref[...]))
    acum = _cumsum_rows(a)
    acum_t = acum.T
    acum_e = _expand_heads(acum, e64_ref)
    acum_b = _expand_heads(acum, e128_ref)
    dt_e = _expand_heads(dt, e64_ref)
    atot_e = acum_e[q - 1:q, :]
    xdt = xs * dt_e
    xdt_b = xdt.astype(BF16)
    xd_b = (xdt * jnp.exp(atot_e - acum_e)).astype(BF16)
    eacum = jnp.exp(acum_e)
    etot = jnp.exp(atot_e)

    tril = lax.broadcasted_iota(I32, (q, q), 0) >= lax.broadcasted_iota(I32, (q, q), 1)
    lane_q = lax.broadcasted_iota(I32, (1, LANES_V7X), 1)
    half_masks = (lane_q < SSD_HEAD_DIM, lane_q >= SSD_HEAD_DIM)
    heads_per_group = SSD_HEADS // SSD_GROUPS

    for g in range(SSD_GROUPS):
        gs = slice(g * gw, (g + 1) * gw)
        b0 = SSD_D_INNER + g * SSD_STATE
        c0 = SSD_D_INNER + SSD_GROUPS * SSD_STATE + g * SSD_STATE
        bg = xa[:, b0:b0 + SSD_STATE]
        cg_b = xa[:, c0:c0 + SSD_STATE].astype(BF16)
        cb = lax.dot_general(cg_b, bg.astype(BF16), (((1,), (1,)), ((), ())),
                             preferred_element_type=F32)
        st = state_ref[g]
        y_off = jnp.dot(cg_b, st.astype(BF16), preferred_element_type=F32) * eacum[:, gs]
        state_ref[g] = st * etot[:, gs] + jnp.dot(bg.T.astype(BF16), xd_b[:, gs],
                                                  preferred_element_type=F32)
        bands = []
        for pr in range(heads_per_group // 2):
            c_lo = g * gw + pr * LANES_V7X
            band = xdt_b[:, c_lo:c_lo + LANES_V7X]
            yb = jnp.zeros((q, LANES_V7X), F32)
            for hh in range(2):
                h = g * heads_per_group + 2 * pr + hh
                seg = acum_b[:, h * LANES_V7X:(h + 1) * LANES_V7X] - acum_t[h:h + 1, :]
                lmat = jnp.where(tril, jnp.exp(seg), 0.0)
                rhs = jnp.where(half_masks[hh], band, jnp.zeros_like(band))
                yb = yb + jnp.dot((cb * lmat).astype(BF16), rhs, preferred_element_type=F32)
            bands.append(yb)
        y = jnp.concatenate(bands, axis=1) + y_off + dskip_ref[:, gs] * xs[:, gs]
        y = y * _silu(z_ref[:, gs])
        y = y * lax.rsqrt(jnp.mean(y * y, axis=-1, keepdims=True) + NORM_EPS) * normw_ref[:, gs]
        y_ref[:, gs] = y.astype(y_ref.dtype)


def _ssd(z, xbc, small, conv_w, conv_b, dt_bias, a_log, f_bias, d_skip, norm_w):
    s = z.shape[0]
    q = SSD_BLOCK
    pad_row = lambda v, lane0: jnp.zeros((1, SMALL_COLS), F32).at[0, lane0:lane0 + v.shape[0]].set(v)
    head_of_col64 = jnp.arange(SSD_D_INNER) // SSD_HEAD_DIM
    head_of_col128 = jnp.arange(SSD_HEADS * LANES_V7X) // LANES_V7X
    rows = jnp.arange(LANES_V7X)[:, None]
    e64 = (rows == head_of_col64[None, :]).astype(BF16)
    e128 = (rows == head_of_col128[None, :]).astype(BF16)
    full = lambda shape: pl.BlockSpec(shape, lambda c: (0,) * len(shape))
    return pl.pallas_call(
        _ssd_kernel,
        out_shape=[jax.ShapeDtypeStruct((s, SSD_D_INNER), BF16),
                   jax.ShapeDtypeStruct((FOX_HEADS, s, LANES_V7X), BF16)],
        grid=(s // q,),
        in_specs=[pl.BlockSpec((q, SSD_D_INNER), lambda c: (c, 0)),
                  pl.BlockSpec((q, SSD_CONV_DIM), lambda c: (c, 0)),
                  pl.BlockSpec((q, SMALL_COLS), lambda c: (c, 0)),
                  full((SSD_CONV, SSD_CONV_DIM)), full((1, SSD_CONV_DIM)),
                  full((1, SMALL_COLS)), full((1, SMALL_COLS)), full((1, SMALL_COLS)),
                  full((1, SSD_D_INNER)), full((1, SSD_D_INNER)),
                  full((LANES_V7X, SSD_D_INNER)), full((LANES_V7X, SSD_HEADS * LANES_V7X))],
        out_specs=[pl.BlockSpec((q, SSD_D_INNER), lambda c: (c, 0)),
                   pl.BlockSpec((FOX_HEADS, q, LANES_V7X), lambda c: (0, c, 0))],
        scratch_shapes=[pltpu.VMEM((q, SSD_CONV_DIM), F32),
                        pltpu.VMEM((SSD_GROUPS, SSD_STATE, SSD_GROUP_WIDTH), F32),
                        pltpu.VMEM((1, SMALL_COLS), F32)],
        compiler_params=_cparams("arbitrary"),
        name="ssd",
    )(z, xbc, small, conv_w, conv_b.reshape(1, -1), pad_row(dt_bias, DT_LANE0),
      pad_row(a_log, DT_LANE0), pad_row(f_bias, FF_LANE0),
      jnp.repeat(d_skip, SSD_HEAD_DIM).reshape(1, -1), norm_w.reshape(1, -1), e64, e128)


def _merge_kernel(ya_ref, yb_ref, yc_ref, wa_ref, wb_ref, wc_ref, ga_ref, gb_ref, gc_ref, o_ref):
    acc = ga_ref[...].astype(F32) * jnp.dot(ya_ref[...], wa_ref[...], preferred_element_type=F32)
    acc += gb_ref[...].astype(F32) * jnp.dot(yb_ref[...], wb_ref[...], preferred_element_type=F32)
    acc += gc_ref[...].astype(F32) * jnp.dot(yc_ref[...], wc_ref[...], preferred_element_type=F32)
    o_ref[...] = acc.astype(o_ref.dtype)


def _merge(ya, yb, yc, wa, wb, wc, gates):
    s = ya.shape[0]
    d = wa.shape[1]
    tn = MERGE_COLS
    nb = d // tn
    lhs = lambda width: pl.BlockSpec((MM_ROWS, width), lambda i, j: (i, 0))
    rhs = lambda width: pl.BlockSpec((width, tn), lambda i, j: (0, j))
    gate = lambda b: pl.BlockSpec((MM_ROWS, tn), lambda i, j: (i, b * nb + j))
    return pl.pallas_call(
        _merge_kernel,
        out_shape=jax.ShapeDtypeStruct((s, d), BF16),
        grid=(s // MM_ROWS, nb),
        in_specs=[lhs(ya.shape[1]), lhs(yb.shape[1]), lhs(yc.shape[1]),
                  rhs(wa.shape[0]), rhs(wb.shape[0]), rhs(wc.shape[0]),
                  gate(0), gate(1), gate(2)],
        out_specs=pl.BlockSpec((MM_ROWS, tn), lambda i, j: (i, j)),
        compiler_params=_cparams("parallel", "parallel"),
        name="merge",
    )(ya, yb, yc, wa, wb, wc, gates, gates, gates)


def _first_argmax_rows(v, iota, sentinel):
    mx = jnp.max(v, axis=0, keepdims=True)
    ix = jnp.min(jnp.where(v == mx, iota, sentinel), axis=0, keepdims=True)
    return mx, ix


def _router_kernel(h_ref, rwt_ref, bias_ref, upper_ref, idx_ref, w_ref, rank_ref, count_ref):
    @pl.when(pl.program_id(0) == 0)
    def _():
        count_ref[...] = jnp.zeros_like(count_ref)

    logits = lax.dot_general(rwt_ref[...], h_ref[...], (((1,), (1,)), ((), ())),
                             preferred_element_type=F32)
    scores = jax.nn.sigmoid(logits)
    biased = scores + bias_ref[...]
    n = logits.shape[1]
    sub = lax.broadcasted_iota(I32, (EXPERTS_PER_GROUP, n), 0)
    group_rows = []
    for g in range(N_EXPERT_GROUPS):
        blk = biased[g * EXPERTS_PER_GROUP:(g + 1) * EXPERTS_PER_GROUP, :]
        top1, i1 = _first_argmax_rows(blk, sub, EXPERTS_PER_GROUP)
        top2 = jnp.max(jnp.where(sub == i1, -jnp.inf, blk), axis=0, keepdims=True)
        group_rows.append(top1 + top2)
    gscore = jnp.concatenate(group_rows, axis=0)
    gi = lax.broadcasted_iota(I32, (N_EXPERT_GROUPS, n), 0)
    gsel = jnp.zeros((N_EXPERT_GROUPS, n), F32)
    for _ in range(TOPK_GROUPS):
        _, ix = _first_argmax_rows(gscore, gi, N_EXPERT_GROUPS)
        hit = gi == ix
        gsel = jnp.where(hit, 1.0, gsel)
        gscore = jnp.where(hit, -jnp.inf, gscore)
    emask = jnp.concatenate(
        [jnp.broadcast_to(gsel[g:g + 1, :], (EXPERTS_PER_GROUP, n)) for g in range(N_EXPERT_GROUPS)],
        axis=0)
    masked = jnp.where(emask > 0.0, biased, -jnp.inf)
    ei = lax.broadcasted_iota(I32, (N_EXPERTS, n), 0)
    idx_rows, w_rows, hits = [], [], []
    for _ in range(TOP_K):
        _, ix = _first_argmax_rows(masked, ei, N_EXPERTS)
        hit = ei == ix
        w_rows.append(jnp.sum(jnp.where(hit, scores, 0.0), axis=0, keepdims=True))
        idx_rows.append(ix)
        hits.append(hit)
        masked = jnp.where(hit, -jnp.inf, masked)
    total = w_rows[0]
    for wk in w_rows[1:]:
        total = total + wk
    pad = SUBLANES_V7X - TOP_K
    idx_ref[...] = jnp.concatenate(idx_rows + [jnp.zeros((pad, n), I32)], axis=0)
    w_ref[...] = jnp.concatenate([wk / total * ROUTED_SCALE for wk in w_rows]
                                 + [jnp.zeros((pad, n), F32)], axis=0)

    chosen = jnp.zeros((N_EXPERTS, n), F32)
    for hit in hits:
        chosen = jnp.where(hit, 1.0, chosen)
    before = jnp.dot(chosen.astype(BF16), upper_ref[...], preferred_element_type=F32)
    before = before + count_ref[:, 0:1]
    rank_rows = [jnp.sum(jnp.where(hit, before, 0.0), axis=0, keepdims=True) for hit in hits]
    rank_ref[...] = jnp.concatenate(rank_rows + [jnp.zeros((pad, n), F32)], axis=0).astype(I32)
    count_ref[...] = count_ref[...] + jnp.sum(chosen, axis=1, keepdims=True)


def _router(h_b, router_w, router_bias):
    t, d = h_b.shape
    rows = ROUTER_ROWS
    out = pl.BlockSpec((SUBLANES_V7X, rows), lambda i: (0, i))
    upper = (jnp.arange(rows)[:, None] < jnp.arange(rows)[None, :]).astype(BF16)
    idx, w, rank, count = pl.pallas_call(
        _router_kernel,
        out_shape=[jax.ShapeDtypeStruct((SUBLANES_V7X, t), I32),
                   jax.ShapeDtypeStruct((SUBLANES_V7X, t), F32),
                   jax.ShapeDtypeStruct((SUBLANES_V7X, t), I32),
                   jax.ShapeDtypeStruct((N_EXPERTS, LANES_V7X), F32)],
        grid=(t // rows,),
        in_specs=[pl.BlockSpec((rows, d), lambda i: (i, 0)),
                  pl.BlockSpec((N_EXPERTS, d), lambda i: (0, 0)),
                  pl.BlockSpec((N_EXPERTS, 1), lambda i: (0, 0)),
                  pl.BlockSpec((rows, rows), lambda i: (0, 0))],
        out_specs=[out, out, out, pl.BlockSpec((N_EXPERTS, LANES_V7X), lambda i: (0, 0))],
        compiler_params=_cparams("arbitrary"),
        name="router",
    )(h_b, router_w.T.astype(BF16), router_bias.reshape(N_EXPERTS, 1), upper)
    return idx[:TOP_K].T, w.T, rank[:TOP_K].T, count[:, 0].astype(I32)


def _route_tiles(counts, n_pairs):
    tm = MOE_ROWS
    n_tiles = n_pairs // tm + N_EXPERTS
    padded = ((counts + tm - 1) // tm) * tm
    off_end = jnp.cumsum(padded)
    off = off_end - padded
    n_used = off_end[-1] // tm
    tiles = jnp.arange(n_tiles, dtype=I32)
    tile_valid = tiles < n_used
    tile_expert = jnp.minimum(jnp.sum((off_end[None, :] <= (tiles * tm)[:, None]).astype(I32), axis=1),
                              N_EXPERTS - 1).astype(I32)
    tile_expert = jnp.where(tile_valid, tile_expert, tile_expert[n_used - 1])
    tile_first = jnp.concatenate([jnp.ones((1,), I32),
                                  (tile_expert[1:] != tile_expert[:-1]).astype(I32)])
    return (off.astype(I32), (off + counts).astype(I32), off_end.astype(I32), tile_expert,
            tile_first, tile_valid.astype(I32), jnp.maximum(n_used - 1, 0).astype(I32).reshape(1))


def _pack_rows(v):
    half = v.shape[1] // 2
    return (_bf16_bits(v[:, :half]) >> 16) | _bf16_bits(v[:, half:])


def _unpack_rows(u):
    lo = pltpu.bitcast(u << 16, F32)
    hi = pltpu.bitcast(u & jnp.uint32(0xFFFF0000), F32)
    return jnp.concatenate([lo, hi], axis=1)


def _dispatch_kernel(fill_lo_ref, fill_hi_ref, last_ref, pos_ref, hp_ref, hb_ref, wg_ref, wu_ref,
                     wd_ref, xs_hbm, shared_ref, zero_ref, sem):
    rows = hp_ref.shape[0]
    tile_rows = zero_ref.shape[0]
    n_tiles = xs_hbm.shape[0] // tile_rows

    def row_to_slot(src_row_ref, slot):
        return pltpu.make_async_copy(src_row_ref, xs_hbm.at[pl.ds(slot, 1), :], sem)

    def zeros_to_tile(tile):
        start = pl.multiple_of(tile * tile_rows, tile_rows)
        return pltpu.make_async_copy(zero_ref, xs_hbm.at[pl.ds(start, tile_rows), :], sem)

    @pl.when(pl.program_id(0) == 0)
    def _():
        zero_ref[...] = jnp.zeros_like(zero_ref)
        zero_row = zero_ref.at[pl.ds(0, 1), :]

        def fill_expert(e, carry):
            lax.fori_loop(fill_lo_ref[e], fill_hi_ref[e],
                          lambda s, c: (row_to_slot(zero_row, s).start(), c)[1], 0)
            return carry

        def drain_expert(e, carry):
            lax.fori_loop(fill_lo_ref[e], fill_hi_ref[e],
                          lambda s, c: (row_to_slot(zero_row, s).wait(), c)[1], 0)
            return carry

        lax.fori_loop(0, N_EXPERTS, fill_expert, 0)
        lax.fori_loop(last_ref[0] + 1, n_tiles, lambda tl, c: (zeros_to_tile(tl).start(), c)[1], 0)
        lax.fori_loop(0, N_EXPERTS, drain_expert, 0)
        lax.fori_loop(last_ref[0] + 1, n_tiles, lambda tl, c: (zeros_to_tile(tl).wait(), c)[1], 0)

    def issue(r, carry):
        src = hp_ref.at[pl.ds(r, 1), :]
        for k in range(TOP_K):
            row_to_slot(src, pos_ref[0, r * TOP_K + k]).start()
        return carry

    def drain(r, carry):
        for k in range(TOP_K):
            row_to_slot(hp_ref.at[pl.ds(0, 1), :], 0).wait()
        return carry

    lax.fori_loop(0, rows, issue, 0)
    h = hb_ref[...]
    act = _silu(jnp.dot(h, wg_ref[...], preferred_element_type=F32)) * jnp.dot(
        h, wu_ref[...], preferred_element_type=F32)
    shared_ref[...] = jnp.dot(act.astype(BF16), wd_ref[...], preferred_element_type=F32)
    lax.fori_loop(0, rows, drain, 0)


def _dispatch(h_packed, h_b, pos_tiles, fill_lo, fill_hi, last_tile, ws_gate, ws_up, ws_down, n_slots):
    t, half = h_packed.shape
    d = h_b.shape[1]
    f = ws_gate.shape[1]
    tm = DISPATCH_ROWS
    const = lambda shape: pl.BlockSpec(shape, lambda i, lo, hi, last: (0,) * len(shape))
    grid_spec = pltpu.PrefetchScalarGridSpec(
        num_scalar_prefetch=3,
        grid=(t // tm,),
        in_specs=[pl.BlockSpec((None, 1, tm * TOP_K), lambda i, lo, hi, last: (i, 0, 0),
                               memory_space=pltpu.SMEM),
                  pl.BlockSpec((tm, half), lambda i, lo, hi, last: (i, 0)),
                  pl.BlockSpec((tm, d), lambda i, lo, hi, last: (i, 0)),
                  const((d, f)), const((d, f)), const((f, d))],
        out_specs=[pl.BlockSpec(memory_space=pl.ANY),
                   pl.BlockSpec((tm, d), lambda i, lo, hi, last: (i, 0))],
        scratch_shapes=[pltpu.VMEM((MOE_ROWS, half), U32), pltpu.SemaphoreType.DMA(())],
    )
    return pl.pallas_call(
        _dispatch_kernel,
        out_shape=[jax.ShapeDtypeStruct((n_slots, half), U32), jax.ShapeDtypeStruct((t, d), F32)],
        grid_spec=grid_spec,
        compiler_params=_cparams("arbitrary"),
        name="moe_dispatch",
    )(fill_lo, fill_hi, last_tile, pos_tiles, h_packed, h_b, ws_gate, ws_up, ws_down)


def _moe_group_kernel(te_ref, tf_ref, tv_ref, last_ref, x_ref, wg_ref, wu_ref, wd_ref, y_ref,
                      wg_b, wu_b, wd_b):
    t = pl.program_id(0)

    @pl.when(tf_ref[t] == 1)
    def _():
        wg_b[...] = wg_ref[...].astype(BF16)
        wu_b[...] = wu_ref[...].astype(BF16)
        wd_b[...] = wd_ref[...].astype(BF16)

    @pl.when(tv_ref[t] == 1)
    def _():
        x = _unpack_rows(x_ref[...]).astype(BF16)
        gate = jnp.dot(x, wg_b[...], preferred_element_type=F32)
        up = jnp.dot(x, wu_b[...], preferred_element_type=F32)
        act = _silu(gate) * up
        y_ref[...] = _pack_rows(jnp.dot(act.astype(BF16), wd_b[...], preferred_element_type=F32))

    @pl.when(tv_ref[t] == 0)
    def _():
        y_ref[...] = jnp.zeros_like(y_ref)


def _moe_grouped(x_sorted, tile_expert, tile_first, tile_valid, last_tile, w_gate, w_up, w_down, layer):
    p, half = x_sorted.shape
    d, f = w_gate.shape[2], w_gate.shape[3]
    tm = MOE_ROWS
    grid_spec = pltpu.PrefetchScalarGridSpec(
        num_scalar_prefetch=4,
        grid=(p // tm,),
        in_specs=[pl.BlockSpec((tm, half), lambda t, te, tf, tv, last: (jnp.minimum(t, last[0]), 0)),
                  pl.BlockSpec((None, None, d, f), lambda t, te, tf, tv, last: (layer, te[t], 0, 0)),
                  pl.BlockSpec((None, None, d, f), lambda t, te, tf, tv, last: (layer, te[t], 0, 0)),
                  pl.BlockSpec((None, None, f, d), lambda t, te, tf, tv, last: (layer, te[t], 0, 0))],
        out_specs=pl.BlockSpec((tm, half), lambda t, te, tf, tv, last: (t, 0)),
        scratch_shapes=[pltpu.VMEM((d, f), BF16), pltpu.VMEM((d, f), BF16), pltpu.VMEM((f, d), BF16)],
    )
    return pl.pallas_call(
        _moe_group_kernel,
        out_shape=jax.ShapeDtypeStruct((p, half), U32),
        grid_spec=grid_spec,
        compiler_params=_cparams("arbitrary"),
        name="moe_grouped",
    )(tile_expert, tile_first, tile_valid, last_tile, x_sorted, w_gate, w_up, w_down)


def _combine_kernel(pos_ref, y_hbm, w_ref, shared_ref, x_ref, gate_ref, o_ref, buf_ref, sem):
    rows = x_ref.shape[0]

    def slot_to_row(slot, k, r):
        return pltpu.make_async_copy(y_hbm.at[pl.ds(slot, 1), :], buf_ref.at[k, pl.ds(r, 1), :], sem)

    def issue(r, carry):
        for k in range(TOP_K):
            slot_to_row(pos_ref[0, r * TOP_K + k], k, r).start()
        return carry

    def drain(r, carry):
        for k in range(TOP_K):
            slot_to_row(0, k, r).wait()
        return carry

    lax.fori_loop(0, rows, issue, 0)
    lax.fori_loop(0, rows, drain, 0)
    total = shared_ref[...]
    w = w_ref[...]
    for k in range(TOP_K):
        total = total + w[:, k:k + 1] * _unpack_rows(buf_ref[k])
    o_ref[...] = x_ref[...] + gate_ref[...] * total


def _combine(y_sorted, pos_tiles, wsel, shared, x, gate):
    t, d = x.shape
    half = y_sorted.shape[1]
    tm = DISPATCH_ROWS
    blk = pl.BlockSpec((tm, d), lambda i: (i, 0))
    return pl.pallas_call(
        _combine_kernel,
        out_shape=jax.ShapeDtypeStruct((t, d), F32),
        grid=(t // tm,),
        in_specs=[pl.BlockSpec((None, 1, tm * TOP_K), lambda i: (i, 0, 0), memory_space=pltpu.SMEM),
                  pl.BlockSpec(memory_space=pl.ANY),
                  pl.BlockSpec((tm, SUBLANES_V7X), lambda i: (i, 0)),
                  blk, blk, pl.BlockSpec((1, d), lambda i: (0, 0))],
        out_specs=blk,
        scratch_shapes=[pltpu.VMEM((TOP_K, tm, half), U32), pltpu.SemaphoreType.DMA(())],
        compiler_params=_cparams("arbitrary"),
        name="moe_combine",
    )(pos_tiles, y_sorted, wsel, shared, x, gate)


def _in_proj_weights(w_in):
    sizes = (DIFF_WIDTH, DIFF_WIDTH, DIFF_WIDTH, SSD_D_INNER, SSD_CONV_DIM, SSD_HEADS,
             FOX_WIDTH, FOX_WIDTH, FOX_WIDTH, FOX_HEADS, N_BRANCHES * D_MODEL)
    cuts = [0]
    for sz in sizes:
        cuts.append(cuts[-1] + sz)
    seg = lambda a, b: w_in[:, cuts[a]:cuts[b]].astype(BF16)
    small = jnp.zeros((w_in.shape[0], SMALL_COLS), F32)
    small = small.at[:, DT_LANE0:DT_LANE0 + SSD_HEADS].set(w_in[:, cuts[5]:cuts[6]])
    small = small.at[:, FF_LANE0:FF_LANE0 + FOX_HEADS].set(w_in[:, cuts[9]:cuts[10]])
    return dict(diff_qk=seg(0, 2), diff_v=seg(2, 3), ssd_z=seg(3, 4), ssd_xbc=seg(4, 5),
                small=small.astype(BF16), fox=seg(6, 9), gates=seg(10, 11))


def _mixer(x, mod, cos, sin_signed, layer_idx, p):
    s = x.shape[0]
    sh1, sc1, g1 = mod[0], mod[1], mod[2]
    (h,) = _norm_mod(x, p["norm_mix_w"], sc1, sh1, (BF16,))
    w = _in_proj_weights(p["w_in"])
    qscale = HEAD_DIM ** -0.5 * LOG2E

    ones = jnp.ones((1, DIFF_WIDTH), F32)
    diff_qk = _matmul(h, w["diff_qk"], BF16, "rope",
                      (cos, sin_signed, jnp.concatenate([ones * qscale, ones], axis=1)), name="proj_diff_qk")
    diff_v = _matmul(h, w["diff_v"], BF16, name="proj_diff_v")
    ssd_z = _matmul(h, w["ssd_z"], F32, name="proj_ssd_z")
    ssd_xbc = _matmul(h, w["ssd_xbc"], F32, name="proj_ssd_xbc")
    small = _matmul(h, w["small"], F32, name="proj_small")
    fox_scale = jnp.concatenate([jnp.full((1, FOX_WIDTH), qscale, F32),
                                 jnp.ones((1, 2 * FOX_WIDTH), F32)], axis=1)
    fox = _matmul(h, w["fox"], BF16, "colscale", (fox_scale,), name="proj_fox")
    gates = _matmul(h, w["gates"], BF16, "sigmoid", name="proj_gates")

    o_diff = _flash(diff_qk, diff_qk, diff_v, n_heads=2 * DIFF_HEADS, q_col0=0, k_col0=DIFF_WIDTH,
                    v_col0=0, dv=DIFF_V_DIM, v_rep=2, mask_shift=int(math.log2(CHUNK)),
                    key_bias=None, out_dtype=F32, name="diff_attn")
    ya = _diff_combine(o_diff, p["diff_lambda"], p["diff_subln_w"], layer_idx)

    yb, key_bias = _ssd(ssd_z, ssd_xbc, small, p["ssd_conv_w"], p["ssd_conv_b"], p["ssd_dt_bias"],
                        p["ssd_a_log"], p["fox_f_bias"], p["ssd_d"], p["ssd_norm_w"])

    yc = _flash(fox, fox, fox, n_heads=FOX_HEADS, q_col0=0, k_col0=FOX_WIDTH, v_col0=2 * FOX_WIDTH,
                dv=HEAD_DIM, v_rep=1, mask_shift=0, key_bias=key_bias, out_dtype=BF16,
                name="fox_attn")

    merged = _merge(ya, yb, yc, p["w_br_diff"].astype(BF16), p["w_br_ssd"].astype(BF16),
                    p["w_br_fox"].astype(BF16), gates)
    return _matmul(merged, p["w_out"].astype(BF16), F32, "residual", (x, g1), name="out_proj")


def _moe(x, mod, p, layer, moe_w_gate, moe_w_up, moe_w_down):
    t = x.shape[0]
    sh2, sc2, g2 = mod[3], mod[4], mod[5]
    h_b, h_packed = _norm_mod(x, p["norm_ffn_w"], sc2, sh2, (BF16, U32))
    eidx, wsel, rank, counts = _router(h_b, p["router_w"], p["router_bias"])
    n_pairs = t * TOP_K
    off, fill_lo, fill_hi, tile_expert, tile_first, tile_valid, last_tile = _route_tiles(counts, n_pairs)
    pos_tiles = (off[eidx] + rank).reshape(t // DISPATCH_ROWS, 1, DISPATCH_ROWS * TOP_K)
    n_slots = n_pairs + N_EXPERTS * MOE_ROWS
    x_sorted, shared = _dispatch(h_packed, h_b, pos_tiles, fill_lo, fill_hi, last_tile,
                                 p["shared_w_gate"].astype(BF16), p["shared_w_up"].astype(BF16),
                                 p["shared_w_down"].astype(BF16), n_slots)
    y_sorted = _moe_grouped(x_sorted, tile_expert, tile_first, tile_valid, last_tile,
                            moe_w_gate, moe_w_up, moe_w_down, layer)
    return _combine(y_sorted, pos_tiles, wsel, shared, x, g2)


_LAYER_PARAMS = ("norm_mix_w", "norm_ffn_w", "w_in", "diff_lambda", "diff_subln_w", "ssd_conv_w",
                 "ssd_conv_b", "ssd_dt_bias", "ssd_a_log", "ssd_d", "ssd_norm_w", "fox_f_bias",
                 "w_br_diff", "w_br_ssd", "w_br_fox", "w_out", "router_w", "router_bias",
                 "shared_w_gate", "shared_w_up", "shared_w_down")


def kernel(x, c, positions, ada_w, ada_b, norm_mix_w, norm_ffn_w, w_in, diff_lambda, diff_subln_w, ssd_conv_w, ssd_conv_b, ssd_dt_bias, ssd_a_log, ssd_d, ssd_norm_w, fox_f_bias, w_br_diff, w_br_ssd, w_br_fox, w_out, router_w, router_bias, moe_w_gate, moe_w_up, moe_w_down, shared_w_gate, shared_w_up, shared_w_down, final_norm_w):
    stacked = dict(norm_mix_w=norm_mix_w, norm_ffn_w=norm_ffn_w, w_in=w_in, diff_lambda=diff_lambda,
                   diff_subln_w=diff_subln_w, ssd_conv_w=ssd_conv_w, ssd_conv_b=ssd_conv_b,
                   ssd_dt_bias=ssd_dt_bias, ssd_a_log=ssd_a_log, ssd_d=ssd_d, ssd_norm_w=ssd_norm_w,
                   fox_f_bias=fox_f_bias, w_br_diff=w_br_diff, w_br_ssd=w_br_ssd, w_br_fox=w_br_fox,
                   w_out=w_out, router_w=router_w, router_bias=router_bias,
                   shared_w_gate=shared_w_gate, shared_w_up=shared_w_up, shared_w_down=shared_w_down)
    batch, seq, d = x.shape
    assert batch == 1 and d == D_MODEL and seq % MM_ROWS == 0
    xs = x.reshape(seq, d)
    mods = _adaln(c, ada_w, ada_b)
    cos, sin_signed = _rope_tables(positions)
    for l in range(DEPTH):
        p = {name: stacked[name][l] for name in _LAYER_PARAMS}
        mod = [mods[l, :, i * d:(i + 1) * d] for i in range(6)]
        xs = _mixer(xs, mod, cos, sin_signed, l, p)
        xs = _moe(xs, mod, p, l, moe_w_gate, moe_w_up, moe_w_down)
    zero = jnp.zeros((1, d), F32)
    (out,) = _norm_mod(xs, final_norm_w, zero, zero, (F32,))
    return out.reshape(batch, seq, d)
```

```python
import functools
import math

import jax
import jax.numpy as jnp
from jax import lax
from jax.experimental import pallas as pl
from jax.experimental.pallas import tpu as pltpu

F32 = jnp.float32
BF16 = jnp.bfloat16
I32 = jnp.int32
U32 = jnp.uint32

D_MODEL = 2048
DEPTH = 2
CHUNK = 64
ROPE_THETA = 10000.0
NORM_EPS = 1e-6
DIFF_HEADS = 4
HEAD_DIM = 128
DIFF_V_DIM = 2 * HEAD_DIM
DIFF_WIDTH = DIFF_HEADS * DIFF_V_DIM
SSD_D_INNER = D_MODEL
SSD_HEAD_DIM = 64
SSD_HEADS = SSD_D_INNER // SSD_HEAD_DIM
SSD_GROUPS = 4
SSD_STATE = 128
SSD_CONV = 4
SSD_CONV_DIM = SSD_D_INNER + 2 * SSD_GROUPS * SSD_STATE
SSD_GROUP_WIDTH = SSD_D_INNER // SSD_GROUPS
FOX_HEADS = 8
FOX_WIDTH = FOX_HEADS * HEAD_DIM
N_BRANCHES = 3
N_EXPERTS = 64
TOP_K = 6
N_EXPERT_GROUPS = 8
EXPERTS_PER_GROUP = N_EXPERTS // N_EXPERT_GROUPS
TOPK_GROUPS = 4
D_EXPERT = 512
D_SHARED = 512
ROUTED_SCALE = 2.5
LOG2E = math.log2(math.e)

LANES_V7X = 128
SUBLANES_V7X = 8
VMEM_BYTES_V7X = 64 * 1024 * 1024
VMEM_LIMIT_BYTES = VMEM_BYTES_V7X - 8 * 1024 * 1024

NORM_ROWS = 512
MM_ROWS = 1024
MM_COLS = 1024
MERGE_COLS = 512
ATTN_BLOCK = 1024
SSD_BLOCK = 128
ROUTER_ROWS = 512
MOE_ROWS = 512
DISPATCH_ROWS = 256
ADALN_COLS = 1024
SMALL_COLS = LANES_V7X
DT_LANE0 = 0
FF_LANE0 = SSD_HEADS


def _cparams(*semantics):
    return pltpu.CompilerParams(dimension_semantics=semantics, vmem_limit_bytes=VMEM_LIMIT_BYTES)


def _silu(v):
    return v * jax.nn.sigmoid(v)


def _softplus(v):
    return jnp.maximum(v, 0.0) + jnp.log1p(jnp.exp(-jnp.abs(v)))


def _adaln_kernel(c_ref, w_ref, b_ref, o_ref):
    cond = _silu(c_ref[...]).astype(BF16)
    o_ref[...] = jnp.dot(cond, w_ref[...].astype(BF16), preferred_element_type=F32) + b_ref[...]


def _adaln(c, ada_w, ada_b):
    n_layers, d, n = ada_w.shape
    c8 = jnp.broadcast_to(c.reshape(1, d), (SUBLANES_V7X, d))
    out = pl.pallas_call(
        _adaln_kernel,
        out_shape=jax.ShapeDtypeStruct((n_layers, SUBLANES_V7X, n), F32),
        grid=(n_layers, n // ADALN_COLS),
        in_specs=[pl.BlockSpec((SUBLANES_V7X, d), lambda l, j: (0, 0)),
                  pl.BlockSpec((None, d, ADALN_COLS), lambda l, j: (l, 0, j)),
                  pl.BlockSpec((None, 1, ADALN_COLS), lambda l, j: (l, 0, j))],
        out_specs=pl.BlockSpec((None, SUBLANES_V7X, ADALN_COLS), lambda l, j: (l, 0, j)),
        compiler_params=_cparams("parallel", "parallel"),
        name="adaln",
    )(c8, ada_w, ada_b.reshape(n_layers, 1, n))
    return out[:, 0:1, :]


def _norm_mod_kernel(x_ref, w_ref, sc_ref, sh_ref, *o_refs):
    x = x_ref[...]
    y = x * lax.rsqrt(jnp.mean(x * x, axis=-1, keepdims=True) + NORM_EPS)
    h = (y * w_ref[...]) * (1.0 + sc_ref[...]) + sh_ref[...]
    for o_ref in o_refs:
        o_ref[...] = _pack_rows(h) if o_ref.dtype == U32 else h.astype(o_ref.dtype)


def _norm_mod(x, w, scale, shift, out_dtypes):
    s, d = x.shape
    row = pl.BlockSpec((1, d), lambda i: (0, 0))
    blk = pl.BlockSpec((NORM_ROWS, d), lambda i: (i, 0))
    width = lambda dt: d // 2 if dt == U32 else d
    outs = pl.pallas_call(
        _norm_mod_kernel,
        out_shape=[jax.ShapeDtypeStruct((s, width(dt)), dt) for dt in out_dtypes],
        grid=(s // NORM_ROWS,),
        in_specs=[blk, row, row, row],
        out_specs=[pl.BlockSpec((NORM_ROWS, width(dt)), lambda i: (i, 0)) for dt in out_dtypes],
        compiler_params=_cparams("parallel"),
        name="norm_mod",
    )(x, w.reshape(1, d), scale.reshape(1, d), shift.reshape(1, d))
    return outs


def _rope_rotate(v, cos, sin_signed):
    return v * cos + pltpu.roll(v, HEAD_DIM // 2, 1) * sin_signed


def _matmul_kernel(*refs, epilogue):
    a_ref, b_ref = refs[0], refs[1]
    o_ref = refs[-1]
    acc = jnp.dot(a_ref[...], b_ref[...], preferred_element_type=F32)
    if epilogue == "sigmoid":
        acc = jax.nn.sigmoid(acc)
    elif epilogue == "rope":
        cos_ref, sin_ref, scale_ref = refs[2], refs[3], refs[4]
        cos, sin_signed = cos_ref[...], sin_ref[...]
        parts = [_rope_rotate(acc[:, g * HEAD_DIM:(g + 1) * HEAD_DIM], cos, sin_signed)
                 for g in range(acc.shape[1] // HEAD_DIM)]
        acc = jnp.concatenate(parts, axis=1) * scale_ref[...]
    elif epilogue == "colscale":
        acc = acc * refs[2][...]
    elif epilogue == "residual":
        res_ref, gate_ref = refs[2], refs[3]
        acc = res_ref[...] + gate_ref[...] * acc
    o_ref[...] = acc.astype(o_ref.dtype)


def _matmul(a, b, out_dtype, epilogue="none", extra=(), cols=MM_COLS, name="matmul"):
    m, k = a.shape
    n = b.shape[1]
    tn = min(cols, n)
    in_specs = [pl.BlockSpec((MM_ROWS, k), lambda i, j: (i, 0)),
                pl.BlockSpec((k, tn), lambda i, j: (0, j))]
    row_tile = pl.BlockSpec((1, tn), lambda i, j: (0, j))
    if epilogue == "rope":
        tab = pl.BlockSpec((MM_ROWS, HEAD_DIM), lambda i, j: (i, 0))
        in_specs += [tab, tab, row_tile]
    elif epilogue == "colscale":
        in_specs += [row_tile]
    elif epilogue == "residual":
        in_specs += [pl.BlockSpec((MM_ROWS, tn), lambda i, j: (i, j)), row_tile]
    return pl.pallas_call(
        functools.partial(_matmul_kernel, epilogue=epilogue),
        out_shape=jax.ShapeDtypeStruct((m, n), out_dtype),
        grid=(m // MM_ROWS, n // tn),
        in_specs=in_specs,
        out_specs=pl.BlockSpec((MM_ROWS, tn), lambda i, j: (i, j)),
        compiler_params=_cparams("parallel", "parallel"),
        name=name,
    )(a, b, *extra)


def _rope_table_kernel(pos_ref, freq_ref, sign_ref, cos_ref, sin_ref):
    ang = pos_ref[...].astype(F32) * freq_ref[...]
    cos_ref[...] = jnp.cos(ang)
    sin_ref[...] = jnp.sin(ang) * sign_ref[...]


def _rope_tables(positions):
    s = positions.shape[-1]
    half = HEAD_DIM // 2
    inv_freq = 1.0 / (ROPE_THETA ** (jnp.arange(half, dtype=F32) * 2.0 / HEAD_DIM))
    freq = jnp.concatenate([inv_freq, inv_freq]).reshape(1, HEAD_DIM)
    sign = jnp.concatenate([-jnp.ones((half,), F32), jnp.ones((half,), F32)]).reshape(1, HEAD_DIM)
    row = pl.BlockSpec((1, HEAD_DIM), lambda i: (0, 0))
    tab = pl.BlockSpec((NORM_ROWS, HEAD_DIM), lambda i: (i, 0))
    return pl.pallas_call(
        _rope_table_kernel,
        out_shape=[jax.ShapeDtypeStruct((s, HEAD_DIM), F32)] * 2,
        grid=(s // NORM_ROWS,),
        in_specs=[pl.BlockSpec((NORM_ROWS, 1), lambda i: (i, 0)), row, row],
        out_specs=[tab, tab],
        compiler_params=_cparams("parallel"),
        name="rope_tables",
    )(positions.reshape(s, 1), freq, sign)


BIAS_PIECES = 3


def _flash_kernel(*refs, block, mask_shift, has_bias):
    if has_bias:
        q_ref, k_ref, vt_ref, kb_ref, o_ref, s_scr, m_scr, l_scr, acc_scr = refs
    else:
        q_ref, k_ref, vt_ref, o_ref, s_scr, m_scr, l_scr, acc_scr = refs
        kb_ref = None
    i = pl.program_id(1)
    q = q_ref[...]
    if has_bias:
        lane = lax.broadcasted_iota(I32, (block, HEAD_DIM), 1)
        q = jnp.concatenate([q, jnp.where(lane < BIAS_PIECES, 1.0, 0.0).astype(BF16)], axis=1)

    def scores(j, slot):
        start = pl.multiple_of(j * block, block)
        k = k_ref[pl.ds(start, block), :]
        if has_bias:
            k = jnp.concatenate([k, kb_ref[pl.ds(start, block), :]], axis=1)
        s_scr[slot] = lax.dot_general(k, q, (((1,), (1,)), ((), ())),
                                      preferred_element_type=F32)

    def update(j, slot, masked):
        s = s_scr[slot]
        if masked:
            key = lax.broadcasted_iota(I32, (block, block), 0) >> mask_shift
            qry = lax.broadcasted_iota(I32, (block, block), 1) >> mask_shift
            s = jnp.where(key <= qry, s, -jnp.inf)
        m = m_scr[...]
        m_new = jnp.maximum(m, jnp.max(s, axis=0, keepdims=True))
        alpha = jnp.exp2(m - m_new)
        p = jnp.exp2(s - m_new)
        l_scr[...] = alpha * l_scr[...] + jnp.sum(p, axis=0, keepdims=True)
        acc_scr[...] = alpha * acc_scr[...] + jnp.dot(vt_ref[j], p.astype(BF16),
                                                      preferred_element_type=F32)
        m_scr[...] = m_new

    m_scr[...] = jnp.full(m_scr.shape, -jnp.inf, F32)
    l_scr[...] = jnp.zeros(l_scr.shape, F32)
    acc_scr[...] = jnp.zeros(acc_scr.shape, F32)

    scores(0, 0)

    def pair(jj, carry):
        j = 2 * jj
        scores(j + 1, 1)
        update(j, 0, False)
        scores(j + 2, 0)
        update(j + 1, 1, False)
        return carry

    lax.fori_loop(0, i // 2, pair, 0)

    @pl.when(i % 2 == 0)
    def _():
        update(i, 0, True)

    @pl.when(i % 2 == 1)
    def _():
        scores(i, 1)
        update(i - 1, 0, False)
        update(i, 1, True)

    o_ref[...] = (acc_scr[...] / l_scr[...]).T.astype(o_ref.dtype)


def _flash(q_arr, k_arr, v_arr, *, n_heads, q_col0, k_col0, v_col0, dv, v_rep, mask_shift,
           key_bias, out_dtype, name):
    s = q_arr.shape[0]
    blk = ATTN_BLOCK
    nkb = s // blk
    qb, kb0 = q_col0 // HEAD_DIM, k_col0 // HEAD_DIM
    n_vheads = n_heads // v_rep
    v_t = v_arr[:, v_col0:v_col0 + n_vheads * dv].reshape(nkb, blk, n_vheads, dv).transpose(2, 0, 3, 1)
    resident = dict(pipeline_mode=pl.Buffered(1))
    in_specs = [pl.BlockSpec((blk, HEAD_DIM), lambda h, i: (i, qb + h)),
                pl.BlockSpec((s, HEAD_DIM), lambda h, i: (0, kb0 + h), **resident),
                pl.BlockSpec((None, nkb, dv, blk), lambda h, i: (h // v_rep, 0, 0, 0), **resident)]
    args = [q_arr, k_arr, v_t]
    if key_bias is not None:
        in_specs.append(pl.BlockSpec((None, s, HEAD_DIM), lambda h, i: (h, 0, 0), **resident))
        args.append(key_bias)
    return pl.pallas_call(
        functools.partial(_flash_kernel, block=blk, mask_shift=mask_shift,
                          has_bias=key_bias is not None),
        out_shape=jax.ShapeDtypeStruct((s, n_heads * dv), out_dtype),
        grid=(n_heads, nkb),
        in_specs=in_specs,
        out_specs=pl.BlockSpec((blk, dv), lambda h, i: (i, h)),
        scratch_shapes=[pltpu.VMEM((2, blk, blk), F32), pltpu.VMEM((1, blk), F32),
                        pltpu.VMEM((1, blk), F32), pltpu.VMEM((dv, blk), F32)],
        compiler_params=_cparams("parallel", "arbitrary"),
        name=name,
    )(*args)


def _diff_combine_kernel(o_ref, lam_ref, w_ref, y_ref, *, lam_init):
    lp = lam_ref[...]
    s1 = jnp.sum(lp[0:1] * lp[1:2], axis=-1, keepdims=True)
    s2 = jnp.sum(lp[2:3] * lp[3:4], axis=-1, keepdims=True)
    lam = jnp.exp(s1) - jnp.exp(s2) + lam_init
    w = w_ref[...]
    for h in range(DIFF_HEADS):
        o1 = o_ref[:, (2 * h) * DIFF_V_DIM:(2 * h + 1) * DIFF_V_DIM]
        o2 = o_ref[:, (2 * h + 1) * DIFF_V_DIM:(2 * h + 2) * DIFF_V_DIM]
        d = o1 - lam * o2
        y = d * lax.rsqrt(jnp.mean(d * d, axis=-1, keepdims=True) + NORM_EPS) * w
        y_ref[:, h * DIFF_V_DIM:(h + 1) * DIFF_V_DIM] = (y * (1.0 - lam_init)).astype(y_ref.dtype)


def _diff_combine(o, lam_params, subln_w, layer_idx):
    s = o.shape[0]
    lam_init = 0.8 - 0.6 * math.exp(-0.3 * layer_idx)
    return pl.pallas_call(
        functools.partial(_diff_combine_kernel, lam_init=lam_init),
        out_shape=jax.ShapeDtypeStruct((s, DIFF_WIDTH), BF16),
        grid=(s // NORM_ROWS,),
        in_specs=[pl.BlockSpec((NORM_ROWS, 2 * DIFF_WIDTH), lambda i: (i, 0)),
                  pl.BlockSpec((4, HEAD_DIM), lambda i: (0, 0)),
                  pl.BlockSpec((1, DIFF_V_DIM), lambda i: (0, 0))],
        out_specs=pl.BlockSpec((NORM_ROWS, DIFF_WIDTH), lambda i: (i, 0)),
        compiler_params=_cparams("parallel"),
        name="diff_combine",
    )(o, lam_params, subln_w.reshape(1, DIFF_V_DIM))


def _cumsum_rows(v):
    n = v.shape[0]
    row = lax.broadcasted_iota(I32, (n, 1), 0)
    shift = 1
    while shift < n:
        v = v + jnp.where(row >= shift, pltpu.roll(v, shift, 0), 0.0)
        shift *= 2
    return v


def _bf16_bits(v):
    u = pltpu.bitcast(v, U32)
    return (u + jnp.uint32(0x7FFF) + ((u >> 16) & jnp.uint32(1))) & jnp.uint32(0xFFFF0000)


def _split_bf16x3(v):
    hi = pltpu.bitcast(_bf16_bits(v), F32)
    r1 = v - hi
    mid = pltpu.bitcast(_bf16_bits(r1), F32)
    lo = r1 - mid
    return hi.astype(BF16), mid.astype(BF16), lo.astype(BF16)


def _expand_heads(v, e_ref):
    hi, mid, lo = _split_bf16x3(v)
    e = e_ref[...]
    return (jnp.dot(hi, e, preferred_element_type=F32) + jnp.dot(mid, e, preferred_element_type=F32)
            + jnp.dot(lo, e, preferred_element_type=F32))


def _ssd_kernel(z_ref, xbc_ref, small_ref, convw_ref, convb_ref, dtb_ref, alog_ref, fb_ref,
                dskip_ref, normw_ref, e64_ref, e128_ref, y_ref, kb_ref,
                prev_ref, state_ref, fcarry_ref):
    q = SSD_BLOCK
    gw = SSD_GROUP_WIDTH

    @pl.when(pl.program_id(0) == 0)
    def _():
        prev_ref[...] = jnp.zeros_like(prev_ref)
        state_ref[...] = jnp.zeros_like(state_ref)
        fcarry_ref[...] = jnp.zeros_like(fcarry_ref)

    cur = xbc_ref[...]
    prev = prev_ref[...]
    row = lax.broadcasted_iota(I32, (q, 1), 0)
    conv = cur * convw_ref[SSD_CONV - 1:SSD_CONV, :]
    for k in range(1, SSD_CONV):
        shifted = jnp.where(row < k, pltpu.roll(prev, k, 0), pltpu.roll(cur, k, 0))
        conv = conv + shifted * convw_ref[SSD_CONV - 1 - k:SSD_CONV - k, :]
    prev_ref[...] = cur
    xa = _silu(conv + convb_ref[...])
    xs = xa[:, :SSD_D_INNER]

    small = small_ref[...]
    lane = lax.broadcasted_iota(I32, (1, SMALL_COLS), 1)
    is_dt = (lane >= DT_LANE0) & (lane < DT_LANE0 + SSD_HEADS)
    is_ff = (lane >= FF_LANE0) & (lane < FF_LANE0 + FOX_HEADS)

    logf = jnp.where(is_ff, -_softplus(-(small + fb_ref[...])), 0.0)
    cum = _cumsum_rows(logf) + fcarry_ref[...]
    fcarry_ref[...] = cum[q - 1:q, :]
    hi, mid, lo = (v.astype(F32) for v in _split_bf16x3(cum * (-LOG2E)))
    lane_q = lax.broadcasted_iota(I32, (q, LANES_V7X), 1)
    for h in range(FOX_HEADS):
        col = FF_LANE0 + h
        piece = lambda v: jnp.broadcast_to(v[:, col:col + 1], (q, LANES_V7X))
        kb_ref[h] = jnp.where(lane_q == 0, piece(hi),
                              jnp.where(lane_q == 1, piece(mid),
                                        jnp.where(lane_q == 2, piece(lo), 0.0))).astype(BF16)

    dt = jnp.where(is_dt, _softplus(small + dtb_ref[...]), 0.0)
    a = dt * (-jnp.exp(alog_ref[...]))
    acum = _cumsum_rows(a)
    acum_t = acum.T
    acum_e = _expand_heads(acum, e64_ref)
    acum_b = _expand_heads(acum, e128_ref)
    dt_e = _expand_heads(dt, e64_ref)
    atot_e = acum_e[q - 1:q, :]
    xdt = xs * dt_e
    xdt_b = xdt.astype(BF16)
    xd_b = (xdt * jnp.exp(atot_e - acum_e)).astype(BF16)
    eacum = jnp.exp(acum_e)
    etot = jnp.exp(atot_e)

    tril = lax.broadcasted_iota(I32, (q, q), 0) >= lax.broadcasted_iota(I32, (q, q), 1)
    lane_q = lax.broadcasted_iota(I32, (1, LANES_V7X), 1)
    half_masks = (lane_q < SSD_HEAD_DIM, lane_q >= SSD_HEAD_DIM)
    heads_per_group = SSD_HEADS // SSD_GROUPS

    for g in range(SSD_GROUPS):
        gs = slice(g * gw, (g + 1) * gw)
        b0 = SSD_D_INNER + g * SSD_STATE
        c0 = SSD_D_INNER + SSD_GROUPS * SSD_STATE + g * SSD_STATE
        bg = xa[:, b0:b0 + SSD_STATE]
        cg_b = xa[:, c0:c0 + SSD_STATE].astype(BF16)
        cb = lax.dot_general(cg_b, bg.astype(BF16), (((1,), (1,)), ((), ())),
                             preferred_element_type=F32)
        st = state_ref[g]
        y_off = jnp.dot(cg_b, st.astype(BF16), preferred_element_type=F32) * eacum[:, gs]
        state_ref[g] = st * etot[:, gs] + jnp.dot(bg.T.astype(BF16), xd_b[:, gs],
                                                  preferred_element_type=F32)
        bands = []
        for pr in range(heads_per_group // 2):
            c_lo = g * gw + pr * LANES_V7X
            band = xdt_b[:, c_lo:c_lo + LANES_V7X]
            yb = jnp.zeros((q, LANES_V7X), F32)
            for hh in range(2):
                h = g * heads_per_group + 2 * pr + hh
                seg = acum_b[:, h * LANES_V7X:(h + 1) * LANES_V7X] - acum_t[h:h + 1, :]
                lmat = jnp.where(tril, jnp.exp(seg), 0.0)
                rhs = jnp.where(half_masks[hh], band, jnp.zeros_like(band))
                yb = yb + jnp.dot((cb * lmat).astype(BF16), rhs, preferred_element_type=F32)
            bands.append(yb)
        y = jnp.concatenate(bands, axis=1) + y_off + dskip_ref[:, gs] * xs[:, gs]
        y = y * _silu(z_ref[:, gs])
        y = y * lax.rsqrt(jnp.mean(y * y, axis=-1, keepdims=True) + NORM_EPS) * normw_ref[:, gs]
        y_ref[:, gs] = y.astype(y_ref.dtype)


def _ssd(z, xbc, small, conv_w, conv_b, dt_bias, a_log, f_bias, d_skip, norm_w):
    s = z.shape[0]
    q = SSD_BLOCK
    pad_row = lambda v, lane0: jnp.zeros((1, SMALL_COLS), F32).at[0, lane0:lane0 + v.shape[0]].set(v)
    head_of_col64 = jnp.arange(SSD_D_INNER) // SSD_HEAD_DIM
    head_of_col128 = jnp.arange(SSD_HEADS * LANES_V7X) // LANES_V7X
    rows = jnp.arange(LANES_V7X)[:, None]
    e64 = (rows == head_of_col64[None, :]).astype(BF16)
    e128 = (rows == head_of_col128[None, :]).astype(BF16)
    full = lambda shape: pl.BlockSpec(shape, lambda c: (0,) * len(shape))
    return pl.pallas_call(
        _ssd_kernel,
        out_shape=[jax.ShapeDtypeStruct((s, SSD_D_INNER), BF16),
                   jax.ShapeDtypeStruct((FOX_HEADS, s, LANES_V7X), BF16)],
        grid=(s // q,),
        in_specs=[pl.BlockSpec((q, SSD_D_INNER), lambda c: (c, 0)),
                  pl.BlockSpec((q, SSD_CONV_DIM), lambda c: (c, 0)),
                  pl.BlockSpec((q, SMALL_COLS), lambda c: (c, 0)),
                  full((SSD_CONV, SSD_CONV_DIM)), full((1, SSD_CONV_DIM)),
                  full((1, SMALL_COLS)), full((1, SMALL_COLS)), full((1, SMALL_COLS)),
                  full((1, SSD_D_INNER)), full((1, SSD_D_INNER)),
                  full((LANES_V7X, SSD_D_INNER)), full((LANES_V7X, SSD_HEADS * LANES_V7X))],
        out_specs=[pl.BlockSpec((q, SSD_D_INNER), lambda c: (c, 0)),
                   pl.BlockSpec((FOX_HEADS, q, LANES_V7X), lambda c: (0, c, 0))],
        scratch_shapes=[pltpu.VMEM((q, SSD_CONV_DIM), F32),
                        pltpu.VMEM((SSD_GROUPS, SSD_STATE, SSD_GROUP_WIDTH), F32),
                        pltpu.VMEM((1, SMALL_COLS), F32)],
        compiler_params=_cparams("arbitrary"),
        name="ssd",
    )(z, xbc, small, conv_w, conv_b.reshape(1, -1), pad_row(dt_bias, DT_LANE0),
      pad_row(a_log, DT_LANE0), pad_row(f_bias, FF_LANE0),
      jnp.repeat(d_skip, SSD_HEAD_DIM).reshape(1, -1), norm_w.reshape(1, -1), e64, e128)


def _merge_kernel(ya_ref, yb_ref, yc_ref, wa_ref, wb_ref, wc_ref, ga_ref, gb_ref, gc_ref, o_ref):
    acc = ga_ref[...].astype(F32) * jnp.dot(ya_ref[...], wa_ref[...], preferred_element_type=F32)
    acc += gb_ref[...].astype(F32) * jnp.dot(yb_ref[...], wb_ref[...], preferred_element_type=F32)
    acc += gc_ref[...].astype(F32) * jnp.dot(yc_ref[...], wc_ref[...], preferred_element_type=F32)
    o_ref[...] = acc.astype(o_ref.dtype)


def _merge(ya, yb, yc, wa, wb, wc, gates):
    s = ya.shape[0]
    d = wa.shape[1]
    tn = MERGE_COLS
    nb = d // tn
    lhs = lambda width: pl.BlockSpec((MM_ROWS, width), lambda i, j: (i, 0))
    rhs = lambda width: pl.BlockSpec((width, tn), lambda i, j: (0, j))
    gate = lambda b: pl.BlockSpec((MM_ROWS, tn), lambda i, j: (i, b * nb + j))
    return pl.pallas_call(
        _merge_kernel,
        out_shape=jax.ShapeDtypeStruct((s, d), BF16),
        grid=(s // MM_ROWS, nb),
        in_specs=[lhs(ya.shape[1]), lhs(yb.shape[1]), lhs(yc.shape[1]),
                  rhs(wa.shape[0]), rhs(wb.shape[0]), rhs(wc.shape[0]),
                  gate(0), gate(1), gate(2)],
        out_specs=pl.BlockSpec((MM_ROWS, tn), lambda i, j: (i, j)),
        compiler_params=_cparams("parallel", "parallel"),
        name="merge",
    )(ya, yb, yc, wa, wb, wc, gates, gates, gates)


def _first_argmax_rows(v, iota, sentinel):
    mx = jnp.max(v, axis=0, keepdims=True)
    ix = jnp.min(jnp.where(v == mx, iota, sentinel), axis=0, keepdims=True)
    return mx, ix


def _router_kernel(h_ref, rwt_ref, bias_ref, upper_ref, idx_ref, w_ref, rank_ref, count_ref):
    @pl.when(pl.program_id(0) == 0)
    def _():
        count_ref[...] = jnp.zeros_like(count_ref)

    logits = lax.dot_general(rwt_ref[...], h_ref[...], (((1,), (1,)), ((), ())),
                             preferred_element_type=F32)
    scores = jax.nn.sigmoid(logits)
    biased = scores + bias_ref[...]
    n = logits.shape[1]
    sub = lax.broadcasted_iota(I32, (EXPERTS_PER_GROUP, n), 0)
    group_rows = []
    for g in range(N_EXPERT_GROUPS):
        blk = biased[g * EXPERTS_PER_GROUP:(g + 1) * EXPERTS_PER_GROUP, :]
        top1, i1 = _first_argmax_rows(blk, sub, EXPERTS_PER_GROUP)
        top2 = jnp.max(jnp.where(sub == i1, -jnp.inf, blk), axis=0, keepdims=True)
        group_rows.append(top1 + top2)
    gscore = jnp.concatenate(group_rows, axis=0)
    gi = lax.broadcasted_iota(I32, (N_EXPERT_GROUPS, n), 0)
    gsel = jnp.zeros((N_EXPERT_GROUPS, n), F32)
    for _ in range(TOPK_GROUPS):
        _, ix = _first_argmax_rows(gscore, gi, N_EXPERT_GROUPS)
        hit = gi == ix
        gsel = jnp.where(hit, 1.0, gsel)
        gscore = jnp.where(hit, -jnp.inf, gscore)
    emask = jnp.concatenate(
        [jnp.broadcast_to(gsel[g:g + 1, :], (EXPERTS_PER_GROUP, n)) for g in range(N_EXPERT_GROUPS)],
        axis=0)
    masked = jnp.where(emask > 0.0, biased, -jnp.inf)
    ei = lax.broadcasted_iota(I32, (N_EXPERTS, n), 0)
    idx_rows, w_rows, hits = [], [], []
    for _ in range(TOP_K):
        _, ix = _first_argmax_rows(masked, ei, N_EXPERTS)
        hit = ei == ix
        w_rows.append(jnp.sum(jnp.where(hit, scores, 0.0), axis=0, keepdims=True))
        idx_rows.append(ix)
        hits.append(hit)
        masked = jnp.where(hit, -jnp.inf, masked)
    total = w_rows[0]
    for wk in w_rows[1:]:
        total = total + wk
    pad = SUBLANES_V7X - TOP_K
    idx_ref[...] = jnp.concatenate(idx_rows + [jnp.zeros((pad, n), I32)], axis=0)
    w_ref[...] = jnp.concatenate([wk / total * ROUTED_SCALE for wk in w_rows]
                                 + [jnp.zeros((pad, n), F32)], axis=0)

    chosen = jnp.zeros((N_EXPERTS, n), F32)
    for hit in hits:
        chosen = jnp.where(hit, 1.0, chosen)
    before = jnp.dot(chosen.astype(BF16), upper_ref[...], preferred_element_type=F32)
    before = before + count_ref[:, 0:1]
    rank_rows = [jnp.sum(jnp.where(hit, before, 0.0), axis=0, keepdims=True) for hit in hits]
    rank_ref[...] = jnp.concatenate(rank_rows + [jnp.zeros((pad, n), F32)], axis=0).astype(I32)
    count_ref[...] = count_ref[...] + jnp.sum(chosen, axis=1, keepdims=True)


def _router(h_b, router_w, router_bias):
    t, d = h_b.shape
    rows = ROUTER_ROWS
    out = pl.BlockSpec((SUBLANES_V7X, rows), lambda i: (0, i))
    upper = (jnp.arange(rows)[:, None] < jnp.arange(rows)[None, :]).astype(BF16)
    idx, w, rank, count = pl.pallas_call(
        _router_kernel,
        out_shape=[jax.ShapeDtypeStruct((SUBLANES_V7X, t), I32),
                   jax.ShapeDtypeStruct((SUBLANES_V7X, t), F32),
                   jax.ShapeDtypeStruct((SUBLANES_V7X, t), I32),
                   jax.ShapeDtypeStruct((N_EXPERTS, LANES_V7X), F32)],
        grid=(t // rows,),
        in_specs=[pl.BlockSpec((rows, d), lambda i: (i, 0)),
                  pl.BlockSpec((N_EXPERTS, d), lambda i: (0, 0)),
                  pl.BlockSpec((N_EXPERTS, 1), lambda i: (0, 0)),
                  pl.BlockSpec((rows, rows), lambda i: (0, 0))],
        out_specs=[out, out, out, pl.BlockSpec((N_EXPERTS, LANES_V7X), lambda i: (0, 0))],
        compiler_params=_cparams("arbitrary"),
        name="router",
    )(h_b, router_w.T.astype(BF16), router_bias.reshape(N_EXPERTS, 1), upper)
    return idx[:TOP_K].T, w.T, rank[:TOP_K].T, count[:, 0].astype(I32)


def _route_tiles(counts, n_pairs):
    tm = MOE_ROWS
    n_tiles = n_pairs // tm + N_EXPERTS
    padded = ((counts + tm - 1) // tm) * tm
    off_end = jnp.cumsum(padded)
    off = off_end - padded
    n_used = off_end[-1] // tm
    tiles = jnp.arange(n_tiles, dtype=I32)
    tile_valid = tiles < n_used
    tile_expert = jnp.minimum(jnp.sum((off_end[None, :] <= (tiles * tm)[:, None]).astype(I32), axis=1),
                              N_EXPERTS - 1).astype(I32)
    tile_expert = jnp.where(tile_valid, tile_expert, tile_expert[n_used - 1])
    tile_first = jnp.concatenate([jnp.ones((1,), I32),
                                  (tile_expert[1:] != tile_expert[:-1]).astype(I32)])
    return (off.astype(I32), (off + counts).astype(I32), off_end.astype(I32), tile_expert,
            tile_first, tile_valid.astype(I32), jnp.maximum(n_used - 1, 0).astype(I32).reshape(1))


def _pack_rows(v):
    half = v.shape[1] // 2
    return (_bf16_bits(v[:, :half]) >> 16) | _bf16_bits(v[:, half:])


def _unpack_rows(u):
    lo = pltpu.bitcast(u << 16, F32)
    hi = pltpu.bitcast(u & jnp.uint32(0xFFFF0000), F32)
    return jnp.concatenate([lo, hi], axis=1)


def _dispatch_kernel(fill_lo_ref, fill_hi_ref, last_ref, pos_ref, hp_ref, hb_ref, wg_ref, wu_ref,
                     wd_ref, xs_hbm, shared_ref, zero_ref, sem):
    rows = hp_ref.shape[0]
    tile_rows = zero_ref.shape[0]
    n_tiles = xs_hbm.shape[0] // tile_rows

    def row_to_slot(src_row_ref, slot):
        return pltpu.make_async_copy(src_row_ref, xs_hbm.at[pl.ds(slot, 1), :], sem)

    def zeros_to_tile(tile):
        start = pl.multiple_of(tile * tile_rows, tile_rows)
        return pltpu.make_async_copy(zero_ref, xs_hbm.at[pl.ds(start, tile_rows), :], sem)

    @pl.when(pl.program_id(0) == 0)
    def _():
        zero_ref[...] = jnp.zeros_like(zero_ref)

        def fill_expert(e, carry):
            @pl.when(fill_hi_ref[e] > fill_lo_ref[e])
            def _():
                zeros_to_tile(fill_hi_ref[e] // tile_rows - 1).start()
            return carry

        def drain_expert(e, carry):
            @pl.when(fill_hi_ref[e] > fill_lo_ref[e])
            def _():
                zeros_to_tile(fill_hi_ref[e] // tile_rows - 1).wait()
            return carry

        lax.fori_loop(0, N_EXPERTS, fill_expert, 0)
        lax.fori_loop(last_ref[0] + 1, n_tiles, lambda tl, c: (zeros_to_tile(tl).start(), c)[1], 0)
        lax.fori_loop(0, N_EXPERTS, drain_expert, 0)
        lax.fori_loop(last_ref[0] + 1, n_tiles, lambda tl, c: (zeros_to_tile(tl).wait(), c)[1], 0)

    def issue(r, carry):
        src = hp_ref.at[pl.ds(r, 1), :]
        for k in range(TOP_K):
            row_to_slot(src, pos_ref[0, r * TOP_K + k]).start()
        return carry

    def drain(r, carry):
        for k in range(TOP_K):
            row_to_slot(hp_ref.at[pl.ds(0, 1), :], 0).wait()
        return carry

    lax.fori_loop(0, rows, issue, 0)
    h = hb_ref[...]
    act = _silu(jnp.dot(h, wg_ref[...], preferred_element_type=F32)) * jnp.dot(
        h, wu_ref[...], preferred_element_type=F32)
    shared_ref[...] = jnp.dot(act.astype(BF16), wd_ref[...], preferred_element_type=F32)
    lax.fori_loop(0, rows, drain, 0)


def _dispatch(h_packed, h_b, pos_tiles, fill_lo, fill_hi, last_tile, ws_gate, ws_up, ws_down, n_slots):
    t, half = h_packed.shape
    d = h_b.shape[1]
    f = ws_gate.shape[1]
    tm = DISPATCH_ROWS
    const = lambda shape: pl.BlockSpec(shape, lambda i, lo, hi, last: (0,) * len(shape))
    grid_spec = pltpu.PrefetchScalarGridSpec(
        num_scalar_prefetch=3,
        grid=(t // tm,),
        in_specs=[pl.BlockSpec((None, 1, tm * TOP_K), lambda i, lo, hi, last: (i, 0, 0),
                               memory_space=pltpu.SMEM),
                  pl.BlockSpec((tm, half), lambda i, lo, hi, last: (i, 0)),
                  pl.BlockSpec((tm, d), lambda i, lo, hi, last: (i, 0)),
                  const((d, f)), const((d, f)), const((f, d))],
        out_specs=[pl.BlockSpec(memory_space=pl.ANY),
                   pl.BlockSpec((tm, d), lambda i, lo, hi, last: (i, 0))],
        scratch_shapes=[pltpu.VMEM((MOE_ROWS, half), U32), pltpu.SemaphoreType.DMA(())],
    )
    return pl.pallas_call(
        _dispatch_kernel,
        out_shape=[jax.ShapeDtypeStruct((n_slots, half), U32), jax.ShapeDtypeStruct((t, d), F32)],
        grid_spec=grid_spec,
        compiler_params=_cparams("arbitrary"),
        name="moe_dispatch",
    )(fill_lo, fill_hi, last_tile, pos_tiles, h_packed, h_b, ws_gate, ws_up, ws_down)


def _moe_group_kernel(te_ref, tf_ref, tv_ref, last_ref, x_ref, wg_ref, wu_ref, wd_ref, y_ref,
                      wg_b, wu_b, wd_b):
    t = pl.program_id(0)

    @pl.when(tf_ref[t] == 1)
    def _():
        wg_b[...] = wg_ref[...].astype(BF16)
        wu_b[...] = wu_ref[...].astype(BF16)
        wd_b[...] = wd_ref[...].astype(BF16)

    @pl.when(tv_ref[t] == 1)
    def _():
        x = _unpack_rows(x_ref[...]).astype(BF16)
        gate = jnp.dot(x, wg_b[...], preferred_element_type=F32)
        up = jnp.dot(x, wu_b[...], preferred_element_type=F32)
        act = _silu(gate) * up
        y_ref[...] = _pack_rows(jnp.dot(act.astype(BF16), wd_b[...], preferred_element_type=F32))

    @pl.when(tv_ref[t] == 0)
    def _():
        y_ref[...] = jnp.zeros_like(y_ref)


def _moe_grouped(x_sorted, tile_expert, tile_first, tile_valid, last_tile, w_gate, w_up, w_down, layer):
    p, half = x_sorted.shape
    d, f = w_gate.shape[2], w_gate.shape[3]
    tm = MOE_ROWS
    grid_spec = pltpu.PrefetchScalarGridSpec(
        num_scalar_prefetch=4,
        grid=(p // tm,),
        in_specs=[pl.BlockSpec((tm, half), lambda t, te, tf, tv, last: (jnp.minimum(t, last[0]), 0)),
                  pl.BlockSpec((None, None, d, f), lambda t, te, tf, tv, last: (layer, te[t], 0, 0)),
                  pl.BlockSpec((None, None, d, f), lambda t, te, tf, tv, last: (layer, te[t], 0, 0)),
                  pl.BlockSpec((None, None, f, d), lambda t, te, tf, tv, last: (layer, te[t], 0, 0))],
        out_specs=pl.BlockSpec((tm, half), lambda t, te, tf, tv, last: (t, 0)),
        scratch_shapes=[pltpu.VMEM((d, f), BF16), pltpu.VMEM((d, f), BF16), pltpu.VMEM((f, d), BF16)],
    )
    return pl.pallas_call(
        _moe_group_kernel,
        out_shape=jax.ShapeDtypeStruct((p, half), U32),
        grid_spec=grid_spec,
        compiler_params=_cparams("arbitrary"),
        name="moe_grouped",
    )(tile_expert, tile_first, tile_valid, last_tile, x_sorted, w_gate, w_up, w_down)


def _combine_kernel(pos_ref, y_hbm, w_ref, shared_ref, x_ref, gate_ref, o_ref, buf_ref, sem):
    rows = x_ref.shape[0]

    def slot_to_row(slot, k, r):
        return pltpu.make_async_copy(y_hbm.at[pl.ds(slot, 1), :], buf_ref.at[k, pl.ds(r, 1), :], sem)

    def issue(r, carry):
        for k in range(TOP_K):
            slot_to_row(pos_ref[0, r * TOP_K + k], k, r).start()
        return carry

    def drain(r, carry):
        for k in range(TOP_K):
            slot_to_row(0, k, r).wait()
        return carry

    lax.fori_loop(0, rows, issue, 0)
    lax.fori_loop(0, rows, drain, 0)
    total = shared_ref[...]
    w = w_ref[...]
    for k in range(TOP_K):
        total = total + w[:, k:k + 1] * _unpack_rows(buf_ref[k])
    o_ref[...] = x_ref[...] + gate_ref[...] * total


def _combine(y_sorted, pos_tiles, wsel, shared, x, gate):
    t, d = x.shape
    half = y_sorted.shape[1]
    tm = DISPATCH_ROWS
    blk = pl.BlockSpec((tm, d), lambda i: (i, 0))
    return pl.pallas_call(
        _combine_kernel,
        out_shape=jax.ShapeDtypeStruct((t, d), F32),
        grid=(t // tm,),
        in_specs=[pl.BlockSpec((None, 1, tm * TOP_K), lambda i: (i, 0, 0), memory_space=pltpu.SMEM),
                  pl.BlockSpec(memory_space=pl.ANY),
                  pl.BlockSpec((tm, SUBLANES_V7X), lambda i: (i, 0)),
                  blk, blk, pl.BlockSpec((1, d), lambda i: (0, 0))],
        out_specs=blk,
        scratch_shapes=[pltpu.VMEM((TOP_K, tm, half), U32), pltpu.SemaphoreType.DMA(())],
        compiler_params=_cparams("arbitrary"),
        name="moe_combine",
    )(pos_tiles, y_sorted, wsel, shared, x, gate)


def _in_proj_weights(w_in):
    sizes = (DIFF_WIDTH, DIFF_WIDTH, DIFF_WIDTH, SSD_D_INNER, SSD_CONV_DIM, SSD_HEADS,
             FOX_WIDTH, FOX_WIDTH, FOX_WIDTH, FOX_HEADS, N_BRANCHES * D_MODEL)
    cuts = [0]
    for sz in sizes:
        cuts.append(cuts[-1] + sz)
    seg = lambda a, b: w_in[:, cuts[a]:cuts[b]].astype(BF16)
    small = jnp.zeros((w_in.shape[0], SMALL_COLS), F32)
    small = small.at[:, DT_LANE0:DT_LANE0 + SSD_HEADS].set(w_in[:, cuts[5]:cuts[6]])
    small = small.at[:, FF_LANE0:FF_LANE0 + FOX_HEADS].set(w_in[:, cuts[9]:cuts[10]])
    return dict(diff_qk=seg(0, 2), diff_v=seg(2, 3), ssd_z=seg(3, 4), ssd_xbc=seg(4, 5),
                small=small.astype(BF16), fox=seg(6, 9), gates=seg(10, 11))


def _mixer(x, mod, cos, sin_signed, layer_idx, p):
    s = x.shape[0]
    sh1, sc1, g1 = mod[0], mod[1], mod[2]
    (h,) = _norm_mod(x, p["norm_mix_w"], sc1, sh1, (BF16,))
    w = _in_proj_weights(p["w_in"])
    qscale = HEAD_DIM ** -0.5 * LOG2E

    ones = jnp.ones((1, DIFF_WIDTH), F32)
    diff_qk = _matmul(h, w["diff_qk"], BF16, "rope",
                      (cos, sin_signed, jnp.concatenate([ones * qscale, ones], axis=1)), name="proj_diff_qk")
    diff_v = _matmul(h, w["diff_v"], BF16, name="proj_diff_v")
    ssd_z = _matmul(h, w["ssd_z"], F32, name="proj_ssd_z")
    ssd_xbc = _matmul(h, w["ssd_xbc"], F32, name="proj_ssd_xbc")
    small = _matmul(h, w["small"], F32, name="proj_small")
    fox_scale = jnp.concatenate([jnp.full((1, FOX_WIDTH), qscale, F32),
                                 jnp.ones((1, 2 * FOX_WIDTH), F32)], axis=1)
    fox = _matmul(h, w["fox"], BF16, "colscale", (fox_scale,), name="proj_fox")
    gates = _matmul(h, w["gates"], BF16, "sigmoid", name="proj_gates")

    o_diff = _flash(diff_qk, diff_qk, diff_v, n_heads=2 * DIFF_HEADS, q_col0=0, k_col0=DIFF_WIDTH,
                    v_col0=0, dv=DIFF_V_DIM, v_rep=2, mask_shift=int(math.log2(CHUNK)),
                    key_bias=None, out_dtype=F32, name="diff_attn")
    ya = _diff_combine(o_diff, p["diff_lambda"], p["diff_subln_w"], layer_idx)

    yb, key_bias = _ssd(ssd_z, ssd_xbc, small, p["ssd_conv_w"], p["ssd_conv_b"], p["ssd_dt_bias"],
                        p["ssd_a_log"], p["fox_f_bias"], p["ssd_d"], p["ssd_norm_w"])

    yc = _flash(fox, fox, fox, n_heads=FOX_HEADS, q_col0=0, k_col0=FOX_WIDTH, v_col0=2 * FOX_WIDTH,
                dv=HEAD_DIM, v_rep=1, mask_shift=0, key_bias=key_bias, out_dtype=BF16,
                name="fox_attn")

    merged = _merge(ya, yb, yc, p["w_br_diff"].astype(BF16), p["w_br_ssd"].astype(BF16),
                    p["w_br_fox"].astype(BF16), gates)
    return _matmul(merged, p["w_out"].astype(BF16), F32, "residual", (x, g1), name="out_proj")


def _moe(x, mod, p, layer, moe_w_gate, moe_w_up, moe_w_down):
    t = x.shape[0]
    sh2, sc2, g2 = mod[3], mod[4], mod[5]
    h_b, h_packed = _norm_mod(x, p["norm_ffn_w"], sc2, sh2, (BF16, U32))
    eidx, wsel, rank, counts = _router(h_b, p["router_w"], p["router_bias"])
    n_pairs = t * TOP_K
    off, fill_lo, fill_hi, tile_expert, tile_first, tile_valid, last_tile = _route_tiles(counts, n_pairs)
    pos_tiles = (off[eidx] + rank).reshape(t // DISPATCH_ROWS, 1, DISPATCH_ROWS * TOP_K)
    n_slots = n_pairs + N_EXPERTS * MOE_ROWS
    x_sorted, shared = _dispatch(h_packed, h_b, pos_tiles, fill_lo, fill_hi, last_tile,
                                 p["shared_w_gate"].astype(BF16), p["shared_w_up"].astype(BF16),
                                 p["shared_w_down"].astype(BF16), n_slots)
    y_sorted = _moe_grouped(x_sorted, tile_expert, tile_first, tile_valid, last_tile,
                            moe_w_gate, moe_w_up, moe_w_down, layer)
    return _combine(y_sorted, pos_tiles, wsel, shared, x, g2)


_LAYER_PARAMS = ("norm_mix_w", "norm_ffn_w", "w_in", "diff_lambda", "diff_subln_w", "ssd_conv_w",
                 "ssd_conv_b", "ssd_dt_bias", "ssd_a_log", "ssd_d", "ssd_norm_w", "fox_f_bias",
                 "w_br_diff", "w_br_ssd", "w_br_fox", "w_out", "router_w", "router_bias",
                 "shared_w_gate", "shared_w_up", "shared_w_down")


def kernel(x, c, positions, ada_w, ada_b, norm_mix_w, norm_ffn_w, w_in, diff_lambda, diff_subln_w, ssd_conv_w, ssd_conv_b, ssd_dt_bias, ssd_a_log, ssd_d, ssd_norm_w, fox_f_bias, w_br_diff, w_br_ssd, w_br_fox, w_out, router_w, router_bias, moe_w_gate, moe_w_up, moe_w_down, shared_w_gate, shared_w_up, shared_w_down, final_norm_w):
    stacked = dict(norm_mix_w=norm_mix_w, norm_ffn_w=norm_ffn_w, w_in=w_in, diff_lambda=diff_lambda,
                   diff_subln_w=diff_subln_w, ssd_conv_w=ssd_conv_w, ssd_conv_b=ssd_conv_b,
                   ssd_dt_bias=ssd_dt_bias, ssd_a_log=ssd_a_log, ssd_d=ssd_d, ssd_norm_w=ssd_norm_w,
                   fox_f_bias=fox_f_bias, w_br_diff=w_br_diff, w_br_ssd=w_br_ssd, w_br_fox=w_br_fox,
                   w_out=w_out, router_w=router_w, router_bias=router_bias,
                   shared_w_gate=shared_w_gate, shared_w_up=shared_w_up, shared_w_down=shared_w_down)
    batch, seq, d = x.shape
    assert batch == 1 and d == D_MODEL and seq % MM_ROWS == 0
    xs = x.reshape(seq, d)
    mods = _adaln(c, ada_w, ada_b)
    cos, sin_signed = _rope_tables(positions)
    for l in range(DEPTH):
        p = {name: stacked[name][l] for name in _LAYER_PARAMS}
        mod = [mods[l, :, i * d:(i + 1) * d] for i in range(6)]
        xs = _mixer(xs, mod, cos, sin_signed, l, p)
        xs = _moe(xs, mod, p, l, moe_w_gate, moe_w_up, moe_w_down)
    zero = jnp.zeros((1, d), F32)
    (out,) = _norm_mod(xs, final_norm_w, zero, zero, (F32,))
    return out.reshape(batch, seq, d)
```
